```python
import jax, jax.numpy as jnp
from jax import lax
import numpy as np

D_MODEL = 4096
BATCH = 1
SEQ = 8192
DEPTH = 1

ATTN_HEADS = 16
HEAD_DIM = 128
ATTN_WIDTH = ATTN_HEADS * HEAD_DIM
CONV_CHANNELS = D_MODEL // 2
CONV_KERNEL = 31
MOBA_BLOCK = 256
MOBA_TOPK = 3
Q_CHUNK = 32
D_FF = 11008
FFN_CONV_KERNEL = 3
NORM_EPS = 1e-6
LN_EPS = 1e-5
NEG_INF = -1e30
IN_WIDTH = 3 * ATTN_WIDTH + 2 * CONV_CHANNELS + 2 * D_MODEL

kernel_name = "moba_conformer_gated_hybrid"


def rmsnorm(x, g):
    xf = x.astype(jnp.float32)
    y = xf * lax.rsqrt(jnp.mean(xf * xf, axis=-1, keepdims=True) + NORM_EPS)
    return (y * g.astype(jnp.float32)).astype(x.dtype)


def layernorm(x, g, b):
    xf = x.astype(jnp.float32)
    mu = jnp.mean(xf, axis=-1, keepdims=True)
    var = jnp.mean(jnp.square(xf - mu), axis=-1, keepdims=True)
    y = (xf - mu) * lax.rsqrt(var + LN_EPS)
    return (y * g.astype(jnp.float32) + b.astype(jnp.float32)).astype(x.dtype)


def causal_depthwise_conv(x, w, b):
    k_width, channels = w.shape
    y = lax.conv_general_dilated(
        x, w[:, None, :].astype(x.dtype), window_strides=(1,), padding=[(k_width - 1, 0)],
        dimension_numbers=("NWC", "WIO", "NWC"), feature_group_count=channels)
    return y + b.astype(x.dtype)


def alibi_slopes(n_heads):
    h = jnp.arange(1, n_heads + 1, dtype=jnp.float32)
    return jnp.exp2(-8.0 * h / n_heads)


def moba_attention(q, k, v):
    B, S, H, hd = q.shape
    nb = -(-S // MOBA_BLOCK)
    s_pad = nb * MOBA_BLOCK
    pad = s_pad - S

    def prep(t):
        t = jnp.pad(t, ((0, 0), (0, pad), (0, 0), (0, 0)))
        return t.transpose(0, 2, 1, 3)

    q, k, v = prep(q), prep(k), prep(v)
    k_blk = k.reshape(B, H, nb, MOBA_BLOCK, hd)
    v_blk = v.reshape(B, H, nb, MOBA_BLOCK, hd)
    k_mean = jnp.mean(k_blk.astype(jnp.float32), axis=3)

    pos = jnp.arange(s_pad)
    q_block = pos // MOBA_BLOCK
    gate = jnp.einsum("bhsd,bhnd->bhsn", q.astype(jnp.float32), k_mean)
    past = jnp.arange(nb)[None, :] < q_block[:, None]
    gate = jnp.where(past, gate, NEG_INF)
    topk = min(MOBA_TOPK, nb)
    _, sel = lax.top_k(gate, topk)
    sel_valid = sel < q_block[:, None]

    slopes = alibi_slopes(H)
    scale = hd ** -0.5
    b_ix = jnp.arange(B)[:, None, None, None]
    h_ix = jnp.arange(H)[None, :, None, None]
    r = jnp.arange(MOBA_BLOCK)

    def chunk(c):
        start = c * Q_CHUNK
        q_c = lax.dynamic_slice_in_dim(q, start, Q_CHUNK, axis=2)
        sel_c = lax.dynamic_slice_in_dim(sel, start, Q_CHUNK, axis=2)
        valid_c = lax.dynamic_slice_in_dim(sel_valid, start, Q_CHUNK, axis=2)
        q_pos = start + jnp.arange(Q_CHUNK)
        own = start // MOBA_BLOCK
        k_sel = k_blk[b_ix, h_ix, sel_c]
        v_sel = v_blk[b_ix, h_ix, sel_c]
        k_own = lax.dynamic_index_in_dim(k_blk, own, axis=2, keepdims=False)
        v_own = lax.dynamic_index_in_dim(v_blk, own, axis=2, keepdims=False)

        s_sel = jnp.einsum("bhqd,bhqkrd->bhqkr", q_c, k_sel,
                           preferred_element_type=jnp.float32) * scale
        kpos_sel = sel_c[..., None] * MOBA_BLOCK + r
        dist_sel = (q_pos[:, None, None] - kpos_sel).astype(jnp.float32)
        s_sel = jnp.where(valid_c[..., None],
                          s_sel - slopes[:, None, None, None] * dist_sel, NEG_INF)

        s_own = jnp.einsum("bhqd,bhrd->bhqr", q_c, k_own,
                           preferred_element_type=jnp.float32) * scale
        dist_own = (q_pos[:, None] - (own * MOBA_BLOCK + r)[None, :]).astype(jnp.float32)
        s_own = jnp.where(dist_own >= 0, s_own - slopes[:, None, None] * dist_own, NEG_INF)

        logits = jnp.concatenate([s_sel.reshape(B, H, Q_CHUNK, topk * MOBA_BLOCK), s_own], axis=-1)
        p = jax.nn.softmax(logits, axis=-1).astype(v.dtype)
        p_sel = p[..., :topk * MOBA_BLOCK].reshape(B, H, Q_CHUNK, topk, MOBA_BLOCK)
        p_own = p[..., topk * MOBA_BLOCK:]
        return (jnp.einsum("bhqkr,bhqkrd->bhqd", p_sel, v_sel)
                + jnp.einsum("bhqr,bhrd->bhqd", p_own, v_own))

    out = lax.map(chunk, jnp.arange(s_pad // Q_CHUNK))
    out = out.transpose(1, 2, 0, 3, 4).reshape(B, H, s_pad, hd)[:, :, :S]
    return out.transpose(0, 2, 1, 3).reshape(B, S, H * hd)


def setup_inputs(seed: int = 0) -> dict:
    key = jax.random.key(seed)
    ks = jax.random.split(key, 16)
    f32 = jnp.float32

    def nrm(k, shape, scale):
        return jax.random.normal(k, shape, f32) * scale

    return {
        "x": jax.random.normal(ks[0], (BATCH, SEQ, D_MODEL), f32),
        "g_mix": 1.0 + nrm(ks[1], (D_MODEL,), 0.01),
        "w_in": nrm(ks[2], (D_MODEL, IN_WIDTH), D_MODEL ** -0.5),
        "w_conv_dw": nrm(ks[3], (CONV_KERNEL, CONV_CHANNELS), CONV_KERNEL ** -0.5),
        "b_conv_dw": nrm(ks[4], (CONV_CHANNELS,), 0.01),
        "ln_conv_g": 1.0 + nrm(ks[5], (CONV_CHANNELS,), 0.01),
        "ln_conv_b": nrm(ks[6], (CONV_CHANNELS,), 0.01),
        "w_proj_attn": nrm(ks[7], (ATTN_WIDTH, D_MODEL), ATTN_WIDTH ** -0.5),
        "w_proj_conv": nrm(ks[8], (CONV_CHANNELS, D_MODEL), CONV_CHANNELS ** -0.5),
        "w_out": nrm(ks[9], (D_MODEL, D_MODEL), D_MODEL ** -0.5),
        "g_ffn": 1.0 + nrm(ks[10], (D_MODEL,), 0.01),
        "w_up": nrm(ks[11], (D_MODEL, 2 * D_FF), D_MODEL ** -0.5),
        "w_ffn_dw": nrm(ks[12], (FFN_CONV_KERNEL, 2 * D_FF), FFN_CONV_KERNEL ** -0.5),
        "b_ffn_dw": nrm(ks[13], (2 * D_FF,), 0.01),
        "w_down": nrm(ks[14], (D_FF, D_MODEL), D_FF ** -0.5),
        "g_final": 1.0 + nrm(ks[15], (D_MODEL,), 0.01),
    }


def reference(x, g_mix, w_in, w_conv_dw, b_conv_dw, ln_conv_g, ln_conv_b, w_proj_attn,
              w_proj_conv, w_out, g_ffn, w_up, w_ffn_dw, b_ffn_dw, w_down, g_final):
    B, S, _ = x.shape
    h = x
    for _layer in range(DEPTH):
        u = rmsnorm(h, g_mix)
        proj = u @ w_in.astype(u.dtype)
        splits = np.cumsum([ATTN_WIDTH, ATTN_WIDTH, ATTN_WIDTH, CONV_CHANNELS, CONV_CHANNELS, D_MODEL])
        q, k, v, glu_a, glu_b, gate_attn, gate_conv = jnp.split(proj, splits, axis=-1)

        q = q.reshape(B, S, ATTN_HEADS, HEAD_DIM)
        k = k.reshape(B, S, ATTN_HEADS, HEAD_DIM)
        v = v.reshape(B, S, ATTN_HEADS, HEAD_DIM)
        a = moba_attention(q, k, v) @ w_proj_attn.astype(u.dtype)

        c = glu_a * jax.nn.sigmoid(glu_b)
        c = causal_depthwise_conv(c, w_conv_dw, b_conv_dw)
        c = jax.nn.silu(layernorm(c, ln_conv_g, ln_conv_b))
        c = c @ w_proj_conv.astype(u.dtype)

        merged = jax.nn.sigmoid(gate_attn) * a + jax.nn.sigmoid(gate_conv) * c
        h = h + merged @ w_out.astype(u.dtype)

        f = rmsnorm(h, g_ffn)
        up = causal_depthwise_conv(f @ w_up.astype(f.dtype), w_ffn_dw, b_ffn_dw)
        f_gate, f_val = jnp.split(up, 2, axis=-1)
        h = h + (jax.nn.silu(f_gate) * f_val) @ w_down.astype(f.dtype)
    return rmsnorm(h, g_final)
```

```python
import functools

import jax
import jax.numpy as jnp
from jax import lax
from jax.experimental import pallas as pl
from jax.experimental.pallas import tpu as pltpu

ATTN_HEADS = 16
HEAD_DIM = 128
MOBA_BLOCK = 256
MOBA_TOPK = 3
NORM_EPS = 1e-6
LN_EPS = 1e-5
NEG_INF = -1e30

F32 = jnp.float32
BF16 = jnp.bfloat16

V7X_VMEM_BYTES = 64 * 1024 * 1024
VMEM_LIMIT_BYTES = V7X_VMEM_BYTES - 8 * 1024 * 1024
BF16_SUBLANES = 16
CONV_HALO_ROWS = 32


def _params(*sem):
    return pltpu.CompilerParams(dimension_semantics=sem, vmem_limit_bytes=VMEM_LIMIT_BYTES)


def _dot(a, b):
    return jnp.dot(a, b, preferred_element_type=F32)


def _sigmoid(x):
    return 1.0 / (1.0 + jnp.exp(-x))


def _rmsnorm_kernel(x_ref, g_ref, o_ref):
    x = x_ref[...]
    ms = jnp.mean(x * x, axis=-1, keepdims=True)
    o_ref[...] = (x * lax.rsqrt(ms + NORM_EPS) * g_ref[...]).astype(o_ref.dtype)


def _rmsnorm(x, g, out_dtype, name, rows=256):
    s, d = x.shape
    return pl.pallas_call(
        _rmsnorm_kernel,
        grid=(s // rows,),
        in_specs=[pl.BlockSpec((rows, d), lambda i: (i, 0)),
                  pl.BlockSpec((1, d), lambda i: (0, 0))],
        out_specs=pl.BlockSpec((rows, d), lambda i: (i, 0)),
        out_shape=jax.ShapeDtypeStruct((s, d), out_dtype),
        compiler_params=_params("parallel"),
        name=name,
    )(x, g.reshape(1, d))


def _mm_kernel(a_ref, w_ref, o_ref):
    o_ref[...] = _dot(a_ref[...], w_ref[...]).astype(o_ref.dtype)


def _mm_residual_kernel(a_ref, w_ref, r_ref, o_ref):
    o_ref[...] = r_ref[...] + _dot(a_ref[...], w_ref[...])


def _matmul(a, w, n, col0, tm, tn, out_dtype, name, residual=None):
    m, k = a.shape
    off = col0 // tn
    in_specs = [pl.BlockSpec((tm, k), lambda i, j: (i, 0)),
                pl.BlockSpec((k, tn), lambda i, j: (0, j + off))]
    args = [a, w]
    kern = _mm_kernel
    if residual is not None:
        in_specs.append(pl.BlockSpec((tm, tn), lambda i, j: (i, j)))
        args.append(residual)
        kern = _mm_residual_kernel
    return pl.pallas_call(
        kern,
        grid=(m // tm, n // tn),
        in_specs=in_specs,
        out_specs=pl.BlockSpec((tm, tn), lambda i, j: (i, j)),
        out_shape=jax.ShapeDtypeStruct((m, n), out_dtype),
        compiler_params=_params("parallel", "arbitrary"),
        name=name,
    )(*args)


def _glu_kernel(u_ref, wa_ref, wb_ref, o_ref):
    u = u_ref[...]
    o_ref[...] = _dot(u, wa_ref[...]) * _sigmoid(_dot(u, wb_ref[...]))


def _glu(u, w_in, col_a, col_b, n, tm, tn):
    m, k = u.shape
    oa, ob = col_a // tn, col_b // tn
    return pl.pallas_call(
        _glu_kernel,
        grid=(m // tm, n // tn),
        in_specs=[pl.BlockSpec((tm, k), lambda i, j: (i, 0)),
                  pl.BlockSpec((k, tn), lambda i, j: (0, j + oa)),
                  pl.BlockSpec((k, tn), lambda i, j: (0, j + ob))],
        out_specs=pl.BlockSpec((tm, tn), lambda i, j: (i, j)),
        out_shape=jax.ShapeDtypeStruct((m, n), F32),
        compiler_params=_params("parallel", "arbitrary"),
        name="glu_proj",
    )(u, w_in, w_in)


def _moba_gate_kernel(q_ref, k_ref, o_ref, *, tq):
    s = k_ref.shape[0]
    nb = s // MOBA_BLOCK
    k = k_ref[...].astype(F32).reshape(nb, MOBA_BLOCK, HEAD_DIM)
    k_mean = jnp.sum(k, axis=1) * (1.0 / MOBA_BLOCK)
    q = q_ref[...]
    gate = jnp.zeros((nb, tq), F32)
    rest = k_mean
    for _ in range(3):
        part = rest.astype(BF16)
        gate = gate + lax.dot_general(part, q, (((1,), (1,)), ((), ())), preferred_element_type=F32)
        rest = rest - part.astype(F32)
    blk = lax.broadcasted_iota(jnp.int32, (nb, tq), 0)
    qpos = pl.program_id(1) * tq + lax.broadcasted_iota(jnp.int32, (nb, tq), 1)
    past = blk < qpos // MOBA_BLOCK
    gate = jnp.where(past, gate, NEG_INF)
    picked = jnp.zeros((nb, tq), jnp.bool_)
    for _ in range(min(MOBA_TOPK, nb)):
        top = jnp.max(gate, axis=0, keepdims=True)
        first = jnp.min(jnp.where(gate == top, blk, nb), axis=0, keepdims=True)
        hit = blk == first
        picked = jnp.logical_or(picked, hit)
        gate = jnp.where(hit, -jnp.inf, gate)
    o_ref[0] = jnp.where(jnp.logical_and(picked, past), 0.0, NEG_INF)


def _moba_gate(qkv, tq=2048):
    s = qkv.shape[0]
    nb = s // MOBA_BLOCK
    return pl.pallas_call(
        functools.partial(_moba_gate_kernel, tq=tq),
        grid=(ATTN_HEADS, s // tq),
        in_specs=[pl.BlockSpec((tq, HEAD_DIM), lambda h, i: (i, h)),
                  pl.BlockSpec((s, HEAD_DIM), lambda h, i: (0, ATTN_HEADS + h))],
        out_specs=pl.BlockSpec((1, nb, tq), lambda h, i: (h, 0, i)),
        out_shape=jax.ShapeDtypeStruct((ATTN_HEADS, nb, s), F32),
        compiler_params=_params("parallel", "arbitrary"),
        name="moba_gate",
    )(qkv, qkv)


def _moba_attn_kernel(slopes_ref, q_ref, k_ref, v_ref, bias_ref, o_ref):
    h = pl.program_id(0)
    qb = pl.program_id(1)
    bs = MOBA_BLOCK
    scale = HEAD_DIM ** -0.5
    slope = slopes_ref[h]
    q = q_ref[...]
    nt = (((1,), (1,)), ((), ()))
    tn = (((0,), (0,)), ((), ()))
    d0 = (lax.broadcasted_iota(jnp.int32, (bs, bs), 1)
          - lax.broadcasted_iota(jnp.int32, (bs, bs), 0)).astype(F32)
    alibi0 = -slope * d0

    def kv(kb):
        rows = pl.ds(pl.multiple_of(kb * bs, bs), bs)
        return k_ref[rows, :], v_ref[rows, :]

    k_own, v_own = kv(qb)
    st = lax.dot_general(k_own, q, nt, preferred_element_type=F32) * scale
    st = jnp.where(d0 >= 0, st + alibi0, NEG_INF)
    m = jnp.max(st, axis=0, keepdims=True)
    p = jnp.exp(st - m)
    l = jnp.sum(p, axis=0, keepdims=True)
    acc = lax.dot_general(v_own, p.astype(BF16), tn, preferred_element_type=F32)

    def body(kb, carry):
        m, l, acc = carry
        k_blk, v_blk = kv(kb)
        shift = slope * ((qb - kb) * bs).astype(F32)
        t = lax.dot_general(k_blk, q, nt, preferred_element_type=F32) * scale + (alibi0 - shift)
        r = bias_ref[0, pl.ds(kb, 1), :]
        m_new = jnp.maximum(m, jnp.max(t, axis=0, keepdims=True) + r)
        alpha = jnp.exp(m - m_new)
        p = jnp.exp(t + (r - m_new))
        l = alpha * l + jnp.sum(p, axis=0, keepdims=True)
        acc = alpha * acc + lax.dot_general(v_blk, p.astype(BF16), tn, preferred_element_type=F32)
        return m_new, l, acc

    m, l, acc = lax.fori_loop(0, qb, body, (m, l, acc))
    o_ref[...] = (acc / l).T.astype(o_ref.dtype)


def _moba_attention(qkv, bias):
    s = qkv.shape[0]
    nb = s // MOBA_BLOCK
    h_idx = jnp.arange(1, ATTN_HEADS + 1, dtype=F32)
    slopes = jnp.exp2(-8.0 * h_idx / ATTN_HEADS)
    return pl.pallas_call(
        _moba_attn_kernel,
        grid=(ATTN_HEADS, nb),
        in_specs=[pl.BlockSpec(memory_space=pltpu.SMEM),
                  pl.BlockSpec((MOBA_BLOCK, HEAD_DIM), lambda h, i: (i, h)),
                  pl.BlockSpec((s, HEAD_DIM), lambda h, i: (0, ATTN_HEADS + h)),
                  pl.BlockSpec((s, HEAD_DIM), lambda h, i: (0, 2 * ATTN_HEADS + h)),
                  pl.BlockSpec((1, nb, MOBA_BLOCK), lambda h, i: (h, 0, i))],
        out_specs=pl.BlockSpec((MOBA_BLOCK, HEAD_DIM), lambda h, i: (i, h)),
        out_shape=jax.ShapeDtypeStruct((s, ATTN_HEADS * HEAD_DIM), BF16),
        compiler_params=_params("parallel", "arbitrary"),
        name="moba_attention",
    )(slopes, qkv, qkv, qkv, bias)


def _dwconv_kernel(x_ref, halo_ref, w_ref, b_ref, o_ref, win_ref, *, taps, lane_chunk):
    t = x_ref.shape[0]
    hr = halo_ref.shape[0]
    first = pl.program_id(0) == 0
    win_ref[0:hr, :] = jnp.where(first, 0.0, halo_ref[...])
    win_ref[hr:, :] = x_ref[...]
    for c in range(x_ref.shape[1] // lane_chunk):
        lanes = slice(c * lane_chunk, (c + 1) * lane_chunk)
        acc = jnp.broadcast_to(b_ref[:, lanes], (t, lane_chunk))
        for k in range(taps):
            acc = acc + w_ref[k:k + 1, lanes] * win_ref[pl.ds(hr - (taps - 1) + k, t), lanes]
        o_ref[:, lanes] = acc


def _dwconv(x, w, b, rows=256, cols=512):
    s, c = x.shape
    taps = w.shape[0]
    per = rows // CONV_HALO_ROWS
    return pl.pallas_call(
        functools.partial(_dwconv_kernel, taps=taps, lane_chunk=128),
        grid=(s // rows, c // cols),
        in_specs=[pl.BlockSpec((rows, cols), lambda i, j: (i, j)),
                  pl.BlockSpec((CONV_HALO_ROWS, cols), lambda i, j: (jnp.maximum(i * per - 1, 0), j)),
                  pl.BlockSpec((taps, cols), lambda i, j: (0, j)),
                  pl.BlockSpec((1, cols), lambda i, j: (0, j))],
        out_specs=pl.BlockSpec((rows, cols), lambda i, j: (i, j)),
        out_shape=jax.ShapeDtypeStruct((s, c), F32),
        scratch_shapes=[pltpu.VMEM((rows + CONV_HALO_ROWS, cols), F32)],
        compiler_params=_params("parallel", "parallel"),
        name="conformer_dwconv",
    )(x, x, w, b.reshape(1, c))


def _ln_silu_kernel(x_ref, g_ref, b_ref, o_ref):
    x = x_ref[...]
    mu = jnp.mean(x, axis=-1, keepdims=True)
    xc = x - mu
    var = jnp.mean(xc * xc, axis=-1, keepdims=True)
    y = xc * lax.rsqrt(var + LN_EPS) * g_ref[...] + b_ref[...]
    o_ref[...] = (y * _sigmoid(y)).astype(o_ref.dtype)


def _ln_silu(x, g, b, rows=256):
    s, c = x.shape
    return pl.pallas_call(
        _ln_silu_kernel,
        grid=(s // rows,),
        in_specs=[pl.BlockSpec((rows, c), lambda i: (i, 0)),
                  pl.BlockSpec((1, c), lambda i: (0, 0)),
                  pl.BlockSpec((1, c), lambda i: (0, 0))],
        out_specs=pl.BlockSpec((rows, c), lambda i: (i, 0)),
        out_shape=jax.ShapeDtypeStruct((s, c), BF16),
        compiler_params=_params("parallel"),
        name="conformer_ln_silu",
    )(x, g.reshape(1, c), b.reshape(1, c))


def _merge_kernel(u_ref, a_ref, c_ref, wga_ref, wgc_ref, wpa_ref, wpc_ref, o_ref):
    u = u_ref[...]
    ga = _sigmoid(_dot(u, wga_ref[...]))
    gc = _sigmoid(_dot(u, wgc_ref[...]))
    o_ref[...] = (ga * _dot(a_ref[...], wpa_ref[...]) + gc * _dot(c_ref[...], wpc_ref[...])).astype(o_ref.dtype)


def _merge(u, attn, cact, w_in, col_ga, col_gc, w_pa, w_pc, tm, tn):
    m, d = u.shape
    n = w_pa.shape[1]
    oa, oc = col_ga // tn, col_gc // tn
    return pl.pallas_call(
        _merge_kernel,
        grid=(m // tm, n // tn),
        in_specs=[pl.BlockSpec((tm, d), lambda i, j: (i, 0)),
                  pl.BlockSpec((tm, attn.shape[1]), lambda i, j: (i, 0)),
                  pl.BlockSpec((tm, cact.shape[1]), lambda i, j: (i, 0)),
                  pl.BlockSpec((d, tn), lambda i, j: (0, j + oa)),
                  pl.BlockSpec((d, tn), lambda i, j: (0, j + oc)),
                  pl.BlockSpec((w_pa.shape[0], tn), lambda i, j: (0, j)),
                  pl.BlockSpec((w_pc.shape[0], tn), lambda i, j: (0, j))],
        out_specs=pl.BlockSpec((tm, tn), lambda i, j: (i, j)),
        out_shape=jax.ShapeDtypeStruct((m, n), BF16),
        compiler_params=_params("parallel", "arbitrary"),
        name="gated_merge",
    )(u, attn, cact, w_in, w_in, w_pa, w_pc)


def _ffn_up_kernel(f_ref, fh_ref, wg_ref, wv_ref, cwg_ref, cwv_ref, cbg_ref, cbv_ref, o_ref, win_ref):
    tm = f_ref.shape[0]
    hr = fh_ref.shape[0]
    taps = cwg_ref.shape[0]
    f = f_ref[...]
    fh = fh_ref[...]
    keep = (pl.program_id(0) > 0).astype(F32)

    def conv(w_ref, cw_ref, cb_ref):
        up = _dot(f, w_ref[...])
        win_ref[0:hr, :] = _dot(fh, w_ref[...]) * keep
        win_ref[hr:, :] = up
        y = cw_ref[taps - 1:taps, :] * up + cb_ref[...]
        for k in range(taps - 1):
            y = y + cw_ref[k:k + 1, :] * win_ref[pl.ds(hr - (taps - 1) + k, tm), :]
        return y

    g = conv(wg_ref, cwg_ref, cbg_ref)
    v = conv(wv_ref, cwv_ref, cbv_ref)
    o_ref[...] = (g * _sigmoid(g) * v).astype(o_ref.dtype)


def _ffn_up(f, w_up, w_dw, b_dw, d_ff, tm, tn):
    m, d = f.shape
    taps = w_dw.shape[0]
    off = d_ff // tn
    per = tm // BF16_SUBLANES
    return pl.pallas_call(
        _ffn_up_kernel,
        grid=(m // tm, d_ff // tn),
        in_specs=[pl.BlockSpec((tm, d), lambda i, j: (i, 0)),
                  pl.BlockSpec((BF16_SUBLANES, d), lambda i, j: (jnp.maximum(i * per - 1, 0), 0)),
                  pl.BlockSpec((d, tn), lambda i, j: (0, j)),
                  pl.BlockSpec((d, tn), lambda i, j: (0, j + off)),
                  pl.BlockSpec((taps, tn), lambda i, j: (0, j)),
                  pl.BlockSpec((taps, tn), lambda i, j: (0, j + off)),
                  pl.BlockSpec((1, tn), lambda i, j: (0, j)),
                  pl.BlockSpec((1, tn), lambda i, j: (0, j + off))],
        out_specs=pl.BlockSpec((tm, tn), lambda i, j: (i, j)),
        out_shape=jax.ShapeDtypeStruct((m, d_ff), BF16),
        scratch_shapes=[pltpu.VMEM((tm + BF16_SUBLANES, tn), F32)],
        compiler_params=_params("parallel", "arbitrary"),
        name="convffn_up",
    )(f, f, w_up, w_up, w_dw, w_dw, b_dw.reshape(1, -1), b_dw.reshape(1, -1))


def kernel(x, g_mix, w_in, w_conv_dw, b_conv_dw, ln_conv_g, ln_conv_b, w_proj_attn, w_proj_conv, w_out,
           g_ffn, w_up, w_ffn_dw, b_ffn_dw, w_down, g_final):
    b, s, d = x.shape
    assert b == 1, "single-sequence layer"
    attn_w = ATTN_HEADS * HEAD_DIM
    conv_c = w_conv_dw.shape[1]
    d_ff = w_down.shape[0]
    col_glu_a = 3 * attn_w
    col_glu_b = col_glu_a + conv_c
    col_ga = col_glu_b + conv_c
    col_gc = col_ga + d

    x2 = x.reshape(s, d)
    w_in_b = w_in.astype(BF16)
    w_pa_b = w_proj_attn.astype(BF16)
    w_pc_b = w_proj_conv.astype(BF16)
    w_out_b = w_out.astype(BF16)
    w_up_b = w_up.astype(BF16)
    w_down_b = w_down.astype(BF16)

    u = _rmsnorm(x2, g_mix, BF16, "rmsnorm_mix")
    qkv = _matmul(u, w_in_b, 3 * attn_w, 0, 1024, 1024, BF16, "qkv_proj")
    c0 = _glu(u, w_in_b, col_glu_a, col_glu_b, conv_c, 1024, 512)

    bias = _moba_gate(qkv)
    attn = _moba_attention(qkv, bias)

    cact = _ln_silu(_dwconv(c0, w_conv_dw, b_conv_dw), ln_conv_g, ln_conv_b)

    merged = _merge(u, attn, cact, w_in_b, col_ga, col_gc, w_pa_b, w_pc_b, 512, 512)
    h = _matmul(merged, w_out_b, d, 0, 512, 1024, F32, "out_proj", residual=x2)

    f = _rmsnorm(h, g_ffn, BF16, "rmsnorm_ffn")
    g = _ffn_up(f, w_up_b, w_ffn_dw, b_ffn_dw, d_ff, 1024, 256)
    h2 = _matmul(g, w_down_b, d, 0, 512, 512, F32, "ffn_down", residual=h)
    out = _rmsnorm(h2, g_final, F32, "rmsnorm_final")
    return out.reshape(b, s, d)
```

```python
import functools

import jax
import jax.numpy as jnp
from jax import lax
from jax.experimental import pallas as pl
from jax.experimental.pallas import tpu as pltpu

ATTN_HEADS = 16
HEAD_DIM = 128
MOBA_BLOCK = 256
MOBA_TOPK = 3
NORM_EPS = 1e-6
LN_EPS = 1e-5
NEG_INF = -1e30

F32 = jnp.float32
BF16 = jnp.bfloat16

V7X_VMEM_BYTES = 64 * 1024 * 1024
VMEM_LIMIT_BYTES = V7X_VMEM_BYTES - 8 * 1024 * 1024
BF16_SUBLANES = 16
CONV_HALO_ROWS = 32


def _params(*sem):
    return pltpu.CompilerParams(dimension_semantics=sem, vmem_limit_bytes=VMEM_LIMIT_BYTES)


def _dot(a, b):
    return jnp.dot(a, b, preferred_element_type=F32)


def _sigmoid(x):
    return 1.0 / (1.0 + jnp.exp(-x))


def _rmsnorm_kernel(x_ref, g_ref, o_ref):
    x = x_ref[...]
    ms = jnp.mean(x * x, axis=-1, keepdims=True)
    o_ref[...] = (x * lax.rsqrt(ms + NORM_EPS) * g_ref[...]).astype(o_ref.dtype)


def _rmsnorm(x, g, out_dtype, name, rows=256):
    s, d = x.shape
    return pl.pallas_call(
        _rmsnorm_kernel,
        grid=(s // rows,),
        in_specs=[pl.BlockSpec((rows, d), lambda i: (i, 0)),
                  pl.BlockSpec((1, d), lambda i: (0, 0))],
        out_specs=pl.BlockSpec((rows, d), lambda i: (i, 0)),
        out_shape=jax.ShapeDtypeStruct((s, d), out_dtype),
        compiler_params=_params("parallel"),
        name=name,
    )(x, g.reshape(1, d))


def _mm_kernel(a_ref, w_ref, o_ref):
    o_ref[...] = _dot(a_ref[...], w_ref[...]).astype(o_ref.dtype)


def _mm_residual_kernel(a_ref, w_ref, r_ref, o_ref):
    o_ref[...] = r_ref[...] + _dot(a_ref[...], w_ref[...])


def _matmul(a, w, n, col0, tm, tn, out_dtype, name, residual=None):
    m, k = a.shape
    off = col0 // tn
    in_specs = [pl.BlockSpec((tm, k), lambda i, j: (i, 0)),
                pl.BlockSpec((k, tn), lambda i, j: (0, j + off))]
    args = [a, w]
    kern = _mm_kernel
    if residual is not None:
        in_specs.append(pl.BlockSpec((tm, tn), lambda i, j: (i, j)))
        args.append(residual)
        kern = _mm_residual_kernel
    return pl.pallas_call(
        kern,
        grid=(m // tm, n // tn),
        in_specs=in_specs,
        out_specs=pl.BlockSpec((tm, tn), lambda i, j: (i, j)),
        out_shape=jax.ShapeDtypeStruct((m, n), out_dtype),
        compiler_params=_params("parallel", "arbitrary"),
        name=name,
    )(*args)


def _glu_kernel(u_ref, wa_ref, wb_ref, o_ref):
    u = u_ref[...]
    o_ref[...] = _dot(u, wa_ref[...]) * _sigmoid(_dot(u, wb_ref[...]))


def _glu(u, w_in, col_a, col_b, n, tm, tn):
    m, k = u.shape
    oa, ob = col_a // tn, col_b // tn
    return pl.pallas_call(
        _glu_kernel,
        grid=(m // tm, n // tn),
        in_specs=[pl.BlockSpec((tm, k), lambda i, j: (i, 0)),
                  pl.BlockSpec((k, tn), lambda i, j: (0, j + oa)),
                  pl.BlockSpec((k, tn), lambda i, j: (0, j + ob))],
        out_specs=pl.BlockSpec((tm, tn), lambda i, j: (i, j)),
        out_shape=jax.ShapeDtypeStruct((m, n), F32),
        compiler_params=_params("parallel", "arbitrary"),
        name="glu_proj",
    )(u, w_in, w_in)


def _moba_gate_kernel(q_ref, k_ref, o_ref, *, tq):
    s = k_ref.shape[0]
    nb = s // MOBA_BLOCK
    k = k_ref[...].astype(F32).reshape(nb, MOBA_BLOCK, HEAD_DIM)
    k_mean = jnp.sum(k, axis=1) * (1.0 / MOBA_BLOCK)
    q = q_ref[...]
    gate = jnp.zeros((nb, tq), F32)
    rest = k_mean
    for _ in range(3):
        part = rest.astype(BF16)
        gate = gate + lax.dot_general(part, q, (((1,), (1,)), ((), ())), preferred_element_type=F32)
        rest = rest - part.astype(F32)
    blk = lax.broadcasted_iota(jnp.int32, (nb, tq), 0)
    qpos = pl.program_id(1) * tq + lax.broadcasted_iota(jnp.int32, (nb, tq), 1)
    past = blk < qpos // MOBA_BLOCK
    gate = jnp.where(past, gate, NEG_INF)
    picked = jnp.zeros((nb, tq), jnp.bool_)
    for _ in range(min(MOBA_TOPK, nb)):
        top = jnp.max(gate, axis=0, keepdims=True)
        first = jnp.min(jnp.where(gate == top, blk, nb), axis=0, keepdims=True)
        hit = blk == first
        picked = jnp.logical_or(picked, hit)
        gate = jnp.where(hit, -jnp.inf, gate)
    o_ref[0] = jnp.where(jnp.logical_and(picked, past), 0.0, NEG_INF)


def _moba_gate(qkv, tq=2048):
    s = qkv.shape[0]
    nb = s // MOBA_BLOCK
    return pl.pallas_call(
        functools.partial(_moba_gate_kernel, tq=tq),
        grid=(ATTN_HEADS, s // tq),
        in_specs=[pl.BlockSpec((tq, HEAD_DIM), lambda h, i: (i, h)),
                  pl.BlockSpec((s, HEAD_DIM), lambda h, i: (0, ATTN_HEADS + h))],
        out_specs=pl.BlockSpec((1, nb, tq), lambda h, i: (h, 0, i)),
        out_shape=jax.ShapeDtypeStruct((ATTN_HEADS, nb, s), F32),
        compiler_params=_params("parallel", "arbitrary"),
        name="moba_gate",
    )(qkv, qkv)


SCORE_GROUP = 8
PROB_GROUP = 4
ONES_ROWS = 16
HEADS_PER_STEP = 2
LOG2E = 1.4426950408889634


def _moba_attn_kernel(zslope_ref, q_ref, k_ref, kb_ref, vt_ref, bias_ref, o_ref, s_ref):
    qb = pl.program_id(1)
    bs = MOBA_BLOCK
    hd = HEAD_DIM
    c = (hd ** -0.5) * LOG2E
    heads = range(HEADS_PER_STEP)
    zslope = [zslope_ref[pl.program_id(0) * HEADS_PER_STEP + a] for a in heads]
    nt = (((1,), (1,)), ((), ()))
    ones3 = (lax.broadcasted_iota(jnp.int32, (bs, hd), 1) < 3).astype(BF16)
    q2 = [jnp.concatenate([q_ref[:, a * hd:(a + 1) * hd], ones3], axis=1) for a in heads]

    def scores(a, row0, n_rows):
        rows = pl.ds(pl.multiple_of(row0, bs), n_rows)
        k2 = jnp.concatenate([k_ref[rows, a * hd:(a + 1) * hd], kb_ref[a, 0:n_rows, :]], axis=1)
        return lax.dot_general(k2, q2[a], nt, preferred_element_type=F32)

    def pick(a, kb):
        return zslope[a] * (kb - qb).astype(F32) + bias_ref[a, pl.ds(kb, 1), :]

    causal = lax.broadcasted_iota(jnp.int32, (bs, bs), 0) <= lax.broadcasted_iota(jnp.int32, (bs, bs), 1)
    s_own = [jnp.where(causal, scores(a, qb * bs, bs), NEG_INF) for a in heads]
    m = tuple(c * jnp.max(s_own[a], axis=0, keepdims=True) for a in heads)

    def pass1(g, m):
        m = list(m)
        for a in heads:
            s = scores(a, g * (SCORE_GROUP * bs), SCORE_GROUP * bs)
            s_ref[a, pl.ds(pl.multiple_of(g * (SCORE_GROUP * bs), bs), SCORE_GROUP * bs), :] = s
            for b in range(SCORE_GROUP):
                kb = g * SCORE_GROUP + b
                top = jnp.max(s[b * bs:(b + 1) * bs], axis=0, keepdims=True)
                m[a] = jnp.maximum(m[a], c * top + pick(a, kb))
        return tuple(m)

    m = lax.fori_loop(0, (qb + SCORE_GROUP - 1) // SCORE_GROUP, pass1, m)

    acc = tuple(_dot(vt_ref[a, qb], jnp.exp2(c * s_own[a] - m[a]).astype(BF16)) for a in heads)

    def pass2(g, acc):
        acc = list(acc)
        for b in range(PROB_GROUP):
            kb = g * PROB_GROUP + b
            for a in heads:
                s = s_ref[a, pl.ds(pl.multiple_of(kb * bs, bs), bs), :]
                w = pick(a, kb) - m[a]
                acc[a] = acc[a] + _dot(vt_ref[a, kb], jnp.exp2(c * s + w).astype(BF16))
        return tuple(acc)

    acc = lax.fori_loop(0, (qb + PROB_GROUP - 1) // PROB_GROUP, pass2, acc)
    for a in heads:
        o_ref[:, a * hd:(a + 1) * hd] = (acc[a][0:hd] / acc[a][hd:hd + 1]).T.astype(o_ref.dtype)


def _moba_attention(qkv, bias):
    s = qkv.shape[0]
    nb = s // MOBA_BLOCK
    assert nb % SCORE_GROUP == 0 and SCORE_GROUP % PROB_GROUP == 0
    scale = HEAD_DIM ** -0.5
    slopes = jnp.exp2(-8.0 * jnp.arange(1, ATTN_HEADS + 1, dtype=F32) / ATTN_HEADS)
    zslope = slopes * (LOG2E * MOBA_BLOCK)
    rest = slopes[:, None] * jnp.arange(MOBA_BLOCK, dtype=F32)[None, :] / scale
    cols = []
    for _ in range(3):
        part = rest.astype(BF16)
        cols.append(part)
        rest = rest - part.astype(F32)
    kbias = jnp.pad(jnp.stack(cols, axis=-1), ((0, 0), (0, 0), (0, HEAD_DIM - 3)))
    kbias = jnp.tile(kbias, (1, SCORE_GROUP, 1))
    v = qkv[:, 2 * ATTN_HEADS * HEAD_DIM:].reshape(nb, MOBA_BLOCK, ATTN_HEADS, HEAD_DIM)
    vt = jnp.concatenate([v.transpose(2, 0, 3, 1),
                          jnp.ones((ATTN_HEADS, nb, ONES_ROWS, MOBA_BLOCK), BF16)], axis=2)
    hps = HEADS_PER_STEP
    groups = ATTN_HEADS // hps
    return pl.pallas_call(
        _moba_attn_kernel,
        grid=(groups, nb),
        in_specs=[pl.BlockSpec(memory_space=pltpu.SMEM),
                  pl.BlockSpec((MOBA_BLOCK, hps * HEAD_DIM), lambda h, i: (i, h)),
                  pl.BlockSpec((s, hps * HEAD_DIM), lambda h, i: (0, groups + h)),
                  pl.BlockSpec((hps, SCORE_GROUP * MOBA_BLOCK, HEAD_DIM), lambda h, i: (h, 0, 0)),
                  pl.BlockSpec((hps, nb, HEAD_DIM + ONES_ROWS, MOBA_BLOCK), lambda h, i: (h, 0, 0, 0)),
                  pl.BlockSpec((hps, nb, MOBA_BLOCK), lambda h, i: (h, 0, i))],
        out_specs=pl.BlockSpec((MOBA_BLOCK, hps * HEAD_DIM), lambda h, i: (i, h)),
        out_shape=jax.ShapeDtypeStruct((s, ATTN_HEADS * HEAD_DIM), BF16),
        scratch_shapes=[pltpu.VMEM((hps, s, MOBA_BLOCK), F32)],
        compiler_params=_params("parallel", "arbitrary"),
        name="moba_attention",
    )(zslope, qkv, qkv, kbias, vt, bias)


def _dwconv_kernel(x_ref, halo_ref, w_ref, b_ref, o_ref, win_ref, *, taps, lane_chunk):
    t = x_ref.shape[0]
    hr = halo_ref.shape[0]
    first = pl.program_id(0) == 0
    win_ref[0:hr, :] = jnp.where(first, 0.0, halo_ref[...])
    win_ref[hr:, :] = x_ref[...]
    for c in range(x_ref.shape[1] // lane_chunk):
        lanes = slice(c * lane_chunk, (c + 1) * lane_chunk)
        acc = jnp.broadcast_to(b_ref[:, lanes], (t, lane_chunk))
        for k in range(taps):
            acc = acc + w_ref[k:k + 1, lanes] * win_ref[pl.ds(hr - (taps - 1) + k, t), lanes]
        o_ref[:, lanes] = acc


def _dwconv(x, w, b, rows=256, cols=512):
    s, c = x.shape
    taps = w.shape[0]
    per = rows // CONV_HALO_ROWS
    return pl.pallas_call(
        functools.partial(_dwconv_kernel, taps=taps, lane_chunk=128),
        grid=(s // rows, c // cols),
        in_specs=[pl.BlockSpec((rows, cols), lambda i, j: (i, j)),
                  pl.BlockSpec((CONV_HALO_ROWS, cols), lambda i, j: (jnp.maximum(i * per - 1, 0), j)),
                  pl.BlockSpec((taps, cols), lambda i, j: (0, j)),
                  pl.BlockSpec((1, cols), lambda i, j: (0, j))],
        out_specs=pl.BlockSpec((rows, cols), lambda i, j: (i, j)),
        out_shape=jax.ShapeDtypeStruct((s, c), F32),
        scratch_shapes=[pltpu.VMEM((rows + CONV_HALO_ROWS, cols), F32)],
        compiler_params=_params("parallel", "parallel"),
        name="conformer_dwconv",
    )(x, x, w, b.reshape(1, c))


def _ln_silu_kernel(x_ref, g_ref, b_ref, o_ref):
    x = x_ref[...]
    mu = jnp.mean(x, axis=-1, keepdims=True)
    xc = x - mu
    var = jnp.mean(xc * xc, axis=-1, keepdims=True)
    y = xc * lax.rsqrt(var + LN_EPS) * g_ref[...] + b_ref[...]
    o_ref[...] = (y * _sigmoid(y)).astype(o_ref.dtype)


def _ln_silu(x, g, b, rows=256):
    s, c = x.shape
    return pl.pallas_call(
        _ln_silu_kernel,
        grid=(s // rows,),
        in_specs=[pl.BlockSpec((rows, c), lambda i: (i, 0)),
                  pl.BlockSpec((1, c), lambda i: (0, 0)),
                  pl.BlockSpec((1, c), lambda i: (0, 0))],
        out_specs=pl.BlockSpec((rows, c), lambda i: (i, 0)),
        out_shape=jax.ShapeDtypeStruct((s, c), BF16),
        compiler_params=_params("parallel"),
        name="conformer_ln_silu",
    )(x, g.reshape(1, c), b.reshape(1, c))


def _merge_kernel(u_ref, a_ref, c_ref, wga_ref, wgc_ref, wpa_ref, wpc_ref, o_ref):
    u = u_ref[...]
    ga = _sigmoid(_dot(u, wga_ref[...]))
    gc = _sigmoid(_dot(u, wgc_ref[...]))
    o_ref[...] = (ga * _dot(a_ref[...], wpa_ref[...]) + gc * _dot(c_ref[...], wpc_ref[...])).astype(o_ref.dtype)


def _merge(u, attn, cact, w_in, col_ga, col_gc, w_pa, w_pc, tm, tn):
    m, d = u.shape
    n = w_pa.shape[1]
    oa, oc = col_ga // tn, col_gc // tn
    return pl.pallas_call(
        _merge_kernel,
        grid=(m // tm, n // tn),
        in_specs=[pl.BlockSpec((tm, d), lambda i, j: (i, 0)),
                  pl.BlockSpec((tm, attn.shape[1]), lambda i, j: (i, 0)),
                  pl.BlockSpec((tm, cact.shape[1]), lambda i, j: (i, 0)),
                  pl.BlockSpec((d, tn), lambda i, j: (0, j + oa)),
                  pl.BlockSpec((d, tn), lambda i, j: (0, j + oc)),
                  pl.BlockSpec((w_pa.shape[0], tn), lambda i, j: (0, j)),
                  pl.BlockSpec((w_pc.shape[0], tn), lambda i, j: (0, j))],
        out_specs=pl.BlockSpec((tm, tn), lambda i, j: (i, j)),
        out_shape=jax.ShapeDtypeStruct((m, n), BF16),
        compiler_params=_params("parallel", "arbitrary"),
        name="gated_merge",
    )(u, attn, cact, w_in, w_in, w_pa, w_pc)


def _ffn_up_kernel(f_ref, fh_ref, wg_ref, wv_ref, cwg_ref, cwv_ref, cbg_ref, cbv_ref, o_ref, win_ref):
    tm = f_ref.shape[0]
    hr = fh_ref.shape[0]
    taps = cwg_ref.shape[0]
    f = f_ref[...]
    fh = fh_ref[...]
    keep = (pl.program_id(0) > 0).astype(F32)

    def conv(w_ref, cw_ref, cb_ref):
        up = _dot(f, w_ref[...])
        win_ref[0:hr, :] = _dot(fh, w_ref[...]) * keep
        win_ref[hr:, :] = up
        y = cw_ref[taps - 1:taps, :] * up + cb_ref[...]
        for k in range(taps - 1):
            y = y + cw_ref[k:k + 1, :] * win_ref[pl.ds(hr - (taps - 1) + k, tm), :]
        return y

    g = conv(wg_ref, cwg_ref, cbg_ref)
    v = conv(wv_ref, cwv_ref, cbv_ref)
    o_ref[...] = (g * _sigmoid(g) * v).astype(o_ref.dtype)


def _ffn_up(f, w_up, w_dw, b_dw, d_ff, tm, tn):
    m, d = f.shape
    taps = w_dw.shape[0]
    off = d_ff // tn
    per = tm // BF16_SUBLANES
    return pl.pallas_call(
        _ffn_up_kernel,
        grid=(m // tm, d_ff // tn),
        in_specs=[pl.BlockSpec((tm, d), lambda i, j: (i, 0)),
                  pl.BlockSpec((BF16_SUBLANES, d), lambda i, j: (jnp.maximum(i * per - 1, 0), 0)),
                  pl.BlockSpec((d, tn), lambda i, j: (0, j)),
                  pl.BlockSpec((d, tn), lambda i, j: (0, j + off)),
                  pl.BlockSpec((taps, tn), lambda i, j: (0, j)),
                  pl.BlockSpec((taps, tn), lambda i, j: (0, j + off)),
                  pl.BlockSpec((1, tn), lambda i, j: (0, j)),
                  pl.BlockSpec((1, tn), lambda i, j: (0, j + off))],
        out_specs=pl.BlockSpec((tm, tn), lambda i, j: (i, j)),
        out_shape=jax.ShapeDtypeStruct((m, d_ff), BF16),
        scratch_shapes=[pltpu.VMEM((tm + BF16_SUBLANES, tn), F32)],
        compiler_params=_params("parallel", "arbitrary"),
        name="convffn_up",
    )(f, f, w_up, w_up, w_dw, w_dw, b_dw.reshape(1, -1), b_dw.reshape(1, -1))


def kernel(x, g_mix, w_in, w_conv_dw, b_conv_dw, ln_conv_g, ln_conv_b, w_proj_attn, w_proj_conv, w_out,
           g_ffn, w_up, w_ffn_dw, b_ffn_dw, w_down, g_final):
    b, s, d = x.shape
    assert b == 1, "single-sequence layer"
    attn_w = ATTN_HEADS * HEAD_DIM
    conv_c = w_conv_dw.shape[1]
    d_ff = w_down.shape[0]
    col_glu_a = 3 * attn_w
    col_glu_b = col_glu_a + conv_c
    col_ga = col_glu_b + conv_c
    col_gc = col_ga + d

    x2 = x.reshape(s, d)
    w_in_b = w_in.astype(BF16)
    w_pa_b = w_proj_attn.astype(BF16)
    w_pc_b = w_proj_conv.astype(BF16)
    w_out_b = w_out.astype(BF16)
    w_up_b = w_up.astype(BF16)
    w_down_b = w_down.astype(BF16)

    u = _rmsnorm(x2, g_mix, BF16, "rmsnorm_mix")
    qkv = _matmul(u, w_in_b, 3 * attn_w, 0, 1024, 1024, BF16, "qkv_proj")
    c0 = _glu(u, w_in_b, col_glu_a, col_glu_b, conv_c, 1024, 512)

    bias = _moba_gate(qkv)
    attn = _moba_attention(qkv, bias)

    cact = _ln_silu(_dwconv(c0, w_conv_dw, b_conv_dw), ln_conv_g, ln_conv_b)

    merged = _merge(u, attn, cact, w_in_b, col_ga, col_gc, w_pa_b, w_pc_b, 512, 512)
    h = _matmul(merged, w_out_b, d, 0, 512, 1024, F32, "out_proj", residual=x2)

    f = _rmsnorm(h, g_ffn, BF16, "rmsnorm_ffn")
    g = _ffn_up(f, w_up_b, w_ffn_dw, b_ffn_dw, d_ff, 1024, 256)
    h2 = _matmul(g, w_down_b, d, 0, 512, 512, F32, "ffn_down", residual=h)
    out = _rmsnorm(h2, g_final, F32, "rmsnorm_final")
    return out.reshape(b, s, d)
```

```python
import functools

import jax
import jax.numpy as jnp
import numpy as np
from jax import lax
from jax.experimental import pallas as pl
from jax.experimental.pallas import tpu as pltpu

ATTN_HEADS = 16
HEAD_DIM = 128
MOBA_BLOCK = 256
MOBA_TOPK = 3
NORM_EPS = 1e-6
LN_EPS = 1e-5
NEG_INF = -1e30

F32 = jnp.float32
BF16 = jnp.bfloat16

V7X_VMEM_BYTES = 64 * 1024 * 1024
VMEM_LIMIT_BYTES = V7X_VMEM_BYTES - 8 * 1024 * 1024
LANES = 128
F32_SUBLANES = 8
BF16_SUBLANES = 16
CONV_HALO_ROWS = 32


def _params(*sem):
    return pltpu.CompilerParams(dimension_semantics=sem, vmem_limit_bytes=VMEM_LIMIT_BYTES)


def _dot(a, b):
    return jnp.dot(a, b, preferred_element_type=F32)


def _sigmoid(x):
    return 1.0 / (1.0 + jnp.exp(-x))


def _rmsnorm_kernel(x_ref, g_ref, o_ref):
    x = x_ref[...]
    ms = jnp.mean(x * x, axis=-1, keepdims=True)
    o_ref[...] = (x * lax.rsqrt(ms + NORM_EPS) * g_ref[...]).astype(o_ref.dtype)


def _rmsnorm(x, g, out_dtype, name, rows=256):
    s, d = x.shape
    return pl.pallas_call(
        _rmsnorm_kernel,
        grid=(s // rows,),
        in_specs=[pl.BlockSpec((rows, d), lambda i: (i, 0)),
                  pl.BlockSpec((1, d), lambda i: (0, 0))],
        out_specs=pl.BlockSpec((rows, d), lambda i: (i, 0)),
        out_shape=jax.ShapeDtypeStruct((s, d), out_dtype),
        compiler_params=_params("parallel"),
        name=name,
    )(x, g.reshape(1, d))


def _mm_kernel(a_ref, w_ref, o_ref):
    o_ref[...] = _dot(a_ref[...], w_ref[...]).astype(o_ref.dtype)


def _mm_residual_kernel(a_ref, w_ref, r_ref, o_ref):
    o_ref[...] = r_ref[...] + _dot(a_ref[...], w_ref[...])


def _matmul(a, w, n, col0, tm, tn, out_dtype, name, residual=None):
    m, k = a.shape
    off = col0 // tn
    in_specs = [pl.BlockSpec((tm, k), lambda i, j: (i, 0)),
                pl.BlockSpec((k, tn), lambda i, j: (0, j + off))]
    args = [a, w]
    kern = _mm_kernel
    if residual is not None:
        in_specs.append(pl.BlockSpec((tm, tn), lambda i, j: (i, j)))
        args.append(residual)
        kern = _mm_residual_kernel
    return pl.pallas_call(
        kern,
        grid=(m // tm, n // tn),
        in_specs=in_specs,
        out_specs=pl.BlockSpec((tm, tn), lambda i, j: (i, j)),
        out_shape=jax.ShapeDtypeStruct((m, n), out_dtype),
        compiler_params=_params("parallel", "arbitrary"),
        name=name,
    )(*args)


def _glu_kernel(u_ref, wa_ref, wb_ref, o_ref):
    u = u_ref[...]
    o_ref[...] = _dot(u, wa_ref[...]) * _sigmoid(_dot(u, wb_ref[...]))


def _glu(u, w_in, col_a, col_b, n, tm, tn):
    m, k = u.shape
    oa, ob = col_a // tn, col_b // tn
    return pl.pallas_call(
        _glu_kernel,
        grid=(m // tm, n // tn),
        in_specs=[pl.BlockSpec((tm, k), lambda i, j: (i, 0)),
                  pl.BlockSpec((k, tn), lambda i, j: (0, j + oa)),
                  pl.BlockSpec((k, tn), lambda i, j: (0, j + ob))],
        out_specs=pl.BlockSpec((tm, tn), lambda i, j: (i, j)),
        out_shape=jax.ShapeDtypeStruct((m, n), F32),
        compiler_params=_params("parallel", "arbitrary"),
        name="glu_proj",
    )(u, w_in, w_in)


def _moba_gate_kernel(q_ref, k_ref, o_ref, *, tq):
    s = k_ref.shape[0]
    nb = s // MOBA_BLOCK
    k = k_ref[...].astype(F32).reshape(nb, MOBA_BLOCK, HEAD_DIM)
    k_mean = jnp.sum(k, axis=1) * (1.0 / MOBA_BLOCK)
    q = q_ref[...]
    gate = jnp.zeros((nb, tq), F32)
    rest = k_mean
    for _ in range(3):
        part = rest.astype(BF16)
        gate = gate + lax.dot_general(part, q, (((1,), (1,)), ((), ())), preferred_element_type=F32)
        rest = rest - part.astype(F32)
    blk = lax.broadcasted_iota(jnp.int32, (nb, tq), 0)
    qpos = pl.program_id(1) * tq + lax.broadcasted_iota(jnp.int32, (nb, tq), 1)
    past = blk < qpos // MOBA_BLOCK
    gate = jnp.where(past, gate, NEG_INF)
    picked = jnp.zeros((nb, tq), jnp.bool_)
    for _ in range(min(MOBA_TOPK, nb)):
        top = jnp.max(gate, axis=0, keepdims=True)
        first = jnp.min(jnp.where(gate == top, blk, nb), axis=0, keepdims=True)
        hit = blk == first
        picked = jnp.logical_or(picked, hit)
        gate = jnp.where(hit, -jnp.inf, gate)
    o_ref[0] = jnp.where(jnp.logical_and(picked, past), 0.0, NEG_INF)


def _moba_gate(qkv, tq=2048):
    s = qkv.shape[0]
    nb = s // MOBA_BLOCK
    return pl.pallas_call(
        functools.partial(_moba_gate_kernel, tq=tq),
        grid=(ATTN_HEADS, s // tq),
        in_specs=[pl.BlockSpec((tq, HEAD_DIM), lambda h, i: (i, h)),
                  pl.BlockSpec((s, HEAD_DIM), lambda h, i: (0, ATTN_HEADS + h))],
        out_specs=pl.BlockSpec((1, nb, tq), lambda h, i: (h, 0, i)),
        out_shape=jax.ShapeDtypeStruct((ATTN_HEADS, nb, s), F32),
        compiler_params=_params("parallel", "arbitrary"),
        name="moba_gate",
    )(qkv, qkv)


SCORE_GROUP = 8
PROB_GROUP = 4
ONES_ROWS = 16
HEADS_PER_STEP = 2
LOG2E = 1.4426950408889634


def _moba_attn_kernel(zslope_ref, q_ref, k_ref, kb_ref, vt_ref, bias_ref, o_ref, s_ref):
    qb = pl.program_id(1)
    bs = MOBA_BLOCK
    hd = HEAD_DIM
    c = (hd ** -0.5) * LOG2E
    heads = range(HEADS_PER_STEP)
    zslope = [zslope_ref[pl.program_id(0) * HEADS_PER_STEP + a] for a in heads]
    nt = (((1,), (1,)), ((), ()))
    ones3 = (lax.broadcasted_iota(jnp.int32, (bs, hd), 1) < 3).astype(BF16)
    q2 = [jnp.concatenate([q_ref[:, a * hd:(a + 1) * hd], ones3], axis=1) for a in heads]

    def scores(a, row0, n_rows):
        rows = pl.ds(pl.multiple_of(row0, bs), n_rows)
        k2 = jnp.concatenate([k_ref[rows, a * hd:(a + 1) * hd], kb_ref[a, 0:n_rows, :]], axis=1)
        return lax.dot_general(k2, q2[a], nt, preferred_element_type=F32)

    def pick(a, kb):
        return zslope[a] * (kb - qb).astype(F32) + bias_ref[a, pl.ds(kb, 1), :]

    causal = lax.broadcasted_iota(jnp.int32, (bs, bs), 0) <= lax.broadcasted_iota(jnp.int32, (bs, bs), 1)
    s_own = [jnp.where(causal, scores(a, qb * bs, bs), NEG_INF) for a in heads]
    m = tuple(c * jnp.max(s_own[a], axis=0, keepdims=True) for a in heads)

    def pass1(g, m):
        m = list(m)
        for a in heads:
            s = scores(a, g * (SCORE_GROUP * bs), SCORE_GROUP * bs)
            s_ref[a, pl.ds(pl.multiple_of(g * (SCORE_GROUP * bs), bs), SCORE_GROUP * bs), :] = s
            for b in range(SCORE_GROUP):
                kb = g * SCORE_GROUP + b
                top = jnp.max(s[b * bs:(b + 1) * bs], axis=0, keepdims=True)
                m[a] = jnp.maximum(m[a], c * top + pick(a, kb))
        return tuple(m)

    m = lax.fori_loop(0, (qb + SCORE_GROUP - 1) // SCORE_GROUP, pass1, m)

    acc = tuple(_dot(vt_ref[a, qb], jnp.exp2(c * s_own[a] - m[a]).astype(BF16)) for a in heads)

    def pass2(g, acc):
        acc = list(acc)
        for b in range(PROB_GROUP):
            kb = g * PROB_GROUP + b
            for a in heads:
                s = s_ref[a, pl.ds(pl.multiple_of(kb * bs, bs), bs), :]
                w = pick(a, kb) - m[a]
                acc[a] = acc[a] + _dot(vt_ref[a, kb], jnp.exp2(c * s + w).astype(BF16))
        return tuple(acc)

    acc = lax.fori_loop(0, (qb + PROB_GROUP - 1) // PROB_GROUP, pass2, acc)
    for a in heads:
        o_ref[:, a * hd:(a + 1) * hd] = (acc[a][0:hd] / acc[a][hd:hd + 1]).T.astype(o_ref.dtype)


def _moba_attention(qkv, bias):
    s = qkv.shape[0]
    nb = s // MOBA_BLOCK
    assert nb % SCORE_GROUP == 0 and SCORE_GROUP % PROB_GROUP == 0
    scale = HEAD_DIM ** -0.5
    slopes = np.exp2(-8.0 * np.arange(1, ATTN_HEADS + 1, dtype=np.float32) / ATTN_HEADS).astype(np.float32)
    zslope = jnp.asarray(slopes * np.float32(LOG2E * MOBA_BLOCK))
    rest = (slopes[:, None] * np.arange(MOBA_BLOCK, dtype=np.float32)[None, :] / np.float32(scale)).astype(np.float32)
    cols = []
    for _ in range(3):
        part = rest.astype(BF16)
        cols.append(part)
        rest = rest - part.astype(np.float32)
    kbias = np.zeros((ATTN_HEADS, MOBA_BLOCK, HEAD_DIM), BF16)
    kbias[:, :, 0:3] = np.stack(cols, axis=-1)
    kbias = jnp.asarray(np.tile(kbias, (1, SCORE_GROUP, 1)))
    v = qkv[:, 2 * ATTN_HEADS * HEAD_DIM:].reshape(nb, MOBA_BLOCK, ATTN_HEADS, HEAD_DIM)
    vt = jnp.concatenate([v.transpose(2, 0, 3, 1),
                          jnp.ones((ATTN_HEADS, nb, ONES_ROWS, MOBA_BLOCK), BF16)], axis=2)
    hps = HEADS_PER_STEP
    groups = ATTN_HEADS // hps
    return pl.pallas_call(
        _moba_attn_kernel,
        grid=(groups, nb),
        in_specs=[pl.BlockSpec(memory_space=pltpu.SMEM),
                  pl.BlockSpec((MOBA_BLOCK, hps * HEAD_DIM), lambda h, i: (i, h)),
                  pl.BlockSpec((s, hps * HEAD_DIM), lambda h, i: (0, groups + h)),
                  pl.BlockSpec((hps, SCORE_GROUP * MOBA_BLOCK, HEAD_DIM), lambda h, i: (h, 0, 0)),
                  pl.BlockSpec((hps, nb, HEAD_DIM + ONES_ROWS, MOBA_BLOCK), lambda h, i: (h, 0, 0, 0)),
                  pl.BlockSpec((hps, nb, MOBA_BLOCK), lambda h, i: (h, 0, i))],
        out_specs=pl.BlockSpec((MOBA_BLOCK, hps * HEAD_DIM), lambda h, i: (i, h)),
        out_shape=jax.ShapeDtypeStruct((s, ATTN_HEADS * HEAD_DIM), BF16),
        scratch_shapes=[pltpu.VMEM((hps, s, MOBA_BLOCK), F32)],
        compiler_params=_params("parallel", "arbitrary"),
        name="moba_attention",
    )(zslope, qkv, qkv, kbias, vt, bias)


def _conv_ln_silu_kernel(x_ref, halo_ref, w_ref, b_ref, g_ref, beta_ref, o_ref, win_ref, y_ref, *, taps):
    i = pl.program_id(0)
    j = pl.program_id(1)
    nj = y_ref.shape[0]
    t, ct = x_ref.shape
    hr = halo_ref.shape[0]
    win_ref[0, 0:hr, :] = jnp.where(i == 0, 0.0, halo_ref[...])
    win_ref[0, hr:, :] = x_ref[...]
    span = hr + t - F32_SUBLANES
    for p in range(1, F32_SUBLANES):
        win_ref[p, 0:span, :] = win_ref[0, pl.ds(p, span), :]
    for c in range(ct // LANES):
        lanes = slice(c * LANES, (c + 1) * LANES)
        acc = jnp.broadcast_to(b_ref[:, lanes], (t, LANES))
        for k in range(taps):
            a, p = divmod(hr - (taps - 1) + k, F32_SUBLANES)
            acc = acc + w_ref[k:k + 1, lanes] * win_ref[p, F32_SUBLANES * a:F32_SUBLANES * a + t, lanes]
        y_ref[j, :, lanes] = acc

    @pl.when(j == nj - 1)
    def _():
        n = nj * ct
        mu = sum(jnp.sum(y_ref[jj], axis=-1, keepdims=True) for jj in range(nj)) * (1.0 / n)
        var = sum(jnp.sum(jnp.square(y_ref[jj] - mu), axis=-1, keepdims=True) for jj in range(nj)) * (1.0 / n)
        inv = lax.rsqrt(var + LN_EPS)
        for jj in range(nj):
            cols = slice(jj * ct, (jj + 1) * ct)
            y = (y_ref[jj] - mu) * inv * g_ref[:, cols] + beta_ref[:, cols]
            o_ref[:, cols] = (y * _sigmoid(y)).astype(o_ref.dtype)


def _conv_ln_silu(x, w, b, g, beta, rows=256, cols=512):
    s, c = x.shape
    taps = w.shape[0]
    assert taps - 1 <= CONV_HALO_ROWS
    per = rows // CONV_HALO_ROWS
    nj = c // cols
    return pl.pallas_call(
        functools.partial(_conv_ln_silu_kernel, taps=taps),
        grid=(s // rows, nj),
        in_specs=[pl.BlockSpec((rows, cols), lambda i, j: (i, j)),
                  pl.BlockSpec((CONV_HALO_ROWS, cols), lambda i, j: (jnp.maximum(i * per - 1, 0), j)),
                  pl.BlockSpec((taps, cols), lambda i, j: (0, j)),
                  pl.BlockSpec((1, cols), lambda i, j: (0, j)),
                  pl.BlockSpec((1, c), lambda i, j: (0, 0)),
                  pl.BlockSpec((1, c), lambda i, j: (0, 0))],
        out_specs=pl.BlockSpec((rows, c), lambda i, j: (i, 0)),
        out_shape=jax.ShapeDtypeStruct((s, c), BF16),
        scratch_shapes=[pltpu.VMEM((F32_SUBLANES, rows + CONV_HALO_ROWS, cols), F32),
                        pltpu.VMEM((nj, rows, cols), F32)],
        compiler_params=_params("parallel", "arbitrary"),
        name="conformer_conv_ln_silu",
    )(x, x, w, b.reshape(1, c), g.reshape(1, c), beta.reshape(1, c))


def _merge_kernel(u_ref, a_ref, c_ref, wga_ref, wgc_ref, wpa_ref, wpc_ref, o_ref):
    u = u_ref[...]
    ga = _sigmoid(_dot(u, wga_ref[...]))
    gc = _sigmoid(_dot(u, wgc_ref[...]))
    o_ref[...] = (ga * _dot(a_ref[...], wpa_ref[...]) + gc * _dot(c_ref[...], wpc_ref[...])).astype(o_ref.dtype)


def _merge(u, attn, cact, w_in, col_ga, col_gc, w_pa, w_pc, tm, tn):
    m, d = u.shape
    n = w_pa.shape[1]
    oa, oc = col_ga // tn, col_gc // tn
    return pl.pallas_call(
        _merge_kernel,
        grid=(m // tm, n // tn),
        in_specs=[pl.BlockSpec((tm, d), lambda i, j: (i, 0)),
                  pl.BlockSpec((tm, attn.shape[1]), lambda i, j: (i, 0)),
                  pl.BlockSpec((tm, cact.shape[1]), lambda i, j: (i, 0)),
                  pl.BlockSpec((d, tn), lambda i, j: (0, j + oa)),
                  pl.BlockSpec((d, tn), lambda i, j: (0, j + oc)),
                  pl.BlockSpec((w_pa.shape[0], tn), lambda i, j: (0, j)),
                  pl.BlockSpec((w_pc.shape[0], tn), lambda i, j: (0, j))],
        out_specs=pl.BlockSpec((tm, tn), lambda i, j: (i, j)),
        out_shape=jax.ShapeDtypeStruct((m, n), BF16),
        compiler_params=_params("parallel", "arbitrary"),
        name="gated_merge",
    )(u, attn, cact, w_in, w_in, w_pa, w_pc)


FFN_ROW_CHUNK = 256


def _ffn_up_kernel(f_ref, fh_ref, wg_ref, wv_ref, cwg_ref, cwv_ref, cbg_ref, cbv_ref, o_ref, win_ref):
    tm = f_ref.shape[0]
    hr = fh_ref.shape[0]
    taps = cwg_ref.shape[0]
    keep = (pl.program_id(0) > 0).astype(F32)
    f = f_ref[...]
    fh = fh_ref[...]

    def conv(z, w_ref, cw_ref, cb_ref):
        up = _dot(f, w_ref[...])
        win_ref[z, 0:hr, :] = _dot(fh, w_ref[...]) * keep
        win_ref[z, hr:, :] = up
        y = cw_ref[taps - 1:taps, :] * up + cb_ref[...]
        for k in range(taps - 1):
            y = y + cw_ref[k:k + 1, :] * win_ref[z, pl.ds(hr - (taps - 1) + k, tm), :]
        return y

    g = conv(0, wg_ref, cwg_ref, cbg_ref)
    act = g * _sigmoid(g)
    v = conv(1, wv_ref, cwv_ref, cbv_ref)
    o_ref[...] = (act * v).astype(o_ref.dtype)


def _ffn_up(f, w_up, w_dw, b_dw, d_ff, tm, tn):
    m, d = f.shape
    taps = w_dw.shape[0]
    off = d_ff // tn
    per = tm // BF16_SUBLANES
    return pl.pallas_call(
        _ffn_up_kernel,
        grid=(m // tm, d_ff // tn),
        in_specs=[pl.BlockSpec((tm, d), lambda i, j: (i, 0)),
                  pl.BlockSpec((BF16_SUBLANES, d), lambda i, j: (jnp.maximum(i * per - 1, 0), 0)),
                  pl.BlockSpec((d, tn), lambda i, j: (0, j)),
                  pl.BlockSpec((d, tn), lambda i, j: (0, j + off)),
                  pl.BlockSpec((taps, tn), lambda i, j: (0, j)),
                  pl.BlockSpec((taps, tn), lambda i, j: (0, j + off)),
                  pl.BlockSpec((1, tn), lambda i, j: (0, j)),
                  pl.BlockSpec((1, tn), lambda i, j: (0, j + off))],
        out_specs=pl.BlockSpec((tm, tn), lambda i, j: (i, j)),
        out_shape=jax.ShapeDtypeStruct((m, d_ff), BF16),
        scratch_shapes=[pltpu.VMEM((2, tm + BF16_SUBLANES, tn), F32)],
        compiler_params=_params("parallel", "arbitrary"),
        name="convffn_up",
    )(f, f, w_up, w_up, w_dw, w_dw, b_dw.reshape(1, -1), b_dw.reshape(1, -1))


def kernel(x, g_mix, w_in, w_conv_dw, b_conv_dw, ln_conv_g, ln_conv_b, w_proj_attn, w_proj_conv, w_out,
           g_ffn, w_up, w_ffn_dw, b_ffn_dw, w_down, g_final):
    b, s, d = x.shape
    assert b == 1, "single-sequence layer"
    attn_w = ATTN_HEADS * HEAD_DIM
    conv_c = w_conv_dw.shape[1]
    d_ff = w_down.shape[0]
    col_glu_a = 3 * attn_w
    col_glu_b = col_glu_a + conv_c
    col_ga = col_glu_b + conv_c
    col_gc = col_ga + d

    x2 = x.reshape(s, d)
    w_in_b = w_in.astype(BF16)
    w_pa_b = w_proj_attn.astype(BF16)
    w_pc_b = w_proj_conv.astype(BF16)
    w_out_b = w_out.astype(BF16)
    w_up_b = w_up.astype(BF16)
    w_down_b = w_down.astype(BF16)

    u = _rmsnorm(x2, g_mix, BF16, "rmsnorm_mix")
    qkv = _matmul(u, w_in_b, 3 * attn_w, 0, 1024, 1024, BF16, "qkv_proj")
    c0 = _glu(u, w_in_b, col_glu_a, col_glu_b, conv_c, 1024, 512)

    bias = _moba_gate(qkv)
    attn = _moba_attention(qkv, bias)

    cact = _conv_ln_silu(c0, w_conv_dw, b_conv_dw, ln_conv_g, ln_conv_b)

    merged = _merge(u, attn, cact, w_in_b, col_ga, col_gc, w_pa_b, w_pc_b, 512, 512)
    h = _matmul(merged, w_out_b, d, 0, 512, 1024, F32, "out_proj", residual=x2)

    f = _rmsnorm(h, g_ffn, BF16, "rmsnorm_ffn")
    g = _ffn_up(f, w_up_b, w_ffn_dw, b_ffn_dw, d_ff, 1024, 256)
    h2 = _matmul(g, w_down_b, d, 0, 512, 512, F32, "ffn_down", residual=h)
    out = _rmsnorm(h2, g_final, F32, "rmsnorm_final")
    return out.reshape(b, s, d)
```

```python
import functools

import jax
import jax.numpy as jnp
import numpy as np
from jax import lax
from jax.experimental import pallas as pl
from jax.experimental.pallas import tpu as pltpu

ATTN_HEADS = 16
HEAD_DIM = 128
MOBA_BLOCK = 256
MOBA_TOPK = 3
NORM_EPS = 1e-6
LN_EPS = 1e-5
NEG_INF = -1e30

F32 = jnp.float32
BF16 = jnp.bfloat16

V7X_VMEM_BYTES = 64 * 1024 * 1024
VMEM_LIMIT_BYTES = V7X_VMEM_BYTES - 8 * 1024 * 1024
LANES = 128
F32_SUBLANES = 8
BF16_SUBLANES = 16
CONV_HALO_ROWS = 32


def _params(*sem):
    return pltpu.CompilerParams(dimension_semantics=sem, vmem_limit_bytes=VMEM_LIMIT_BYTES)


def _dot(a, b):
    return jnp.dot(a, b, preferred_element_type=F32)


def _sigmoid(x):
    return 1.0 / (1.0 + jnp.exp(-x))


def _cast_weights_once(w_refs, wb_refs):
    @pl.when(pl.program_id(1) == 0)
    def _():
        for w_ref, wb_ref in zip(w_refs, wb_refs):
            wb_ref[...] = w_ref[...].astype(BF16)


def _rmsnorm_kernel(x_ref, g_ref, o_ref):
    x = x_ref[...]
    ms = jnp.mean(x * x, axis=-1, keepdims=True)
    o_ref[...] = (x * lax.rsqrt(ms + NORM_EPS) * g_ref[...]).astype(o_ref.dtype)


def _rmsnorm(x, g, out_dtype, name, rows=256):
    s, d = x.shape
    return pl.pallas_call(
        _rmsnorm_kernel,
        grid=(s // rows,),
        in_specs=[pl.BlockSpec((rows, d), lambda i: (i, 0)),
                  pl.BlockSpec((1, d), lambda i: (0, 0))],
        out_specs=pl.BlockSpec((rows, d), lambda i: (i, 0)),
        out_shape=jax.ShapeDtypeStruct((s, d), out_dtype),
        compiler_params=_params("parallel"),
        name=name,
    )(x, g.reshape(1, d))


def _mm_kernel(a_ref, w_ref, o_ref, wb_ref):
    _cast_weights_once([w_ref], [wb_ref])
    o_ref[...] = _dot(a_ref[...], wb_ref[...]).astype(o_ref.dtype)


def _mm_residual_kernel(a_ref, w_ref, r_ref, o_ref, wb_ref):
    _cast_weights_once([w_ref], [wb_ref])
    o_ref[...] = r_ref[...] + _dot(a_ref[...], wb_ref[...])


def _matmul(a, w, n, col0, tm, tn, out_dtype, name, residual=None):
    m, k = a.shape
    off = col0 // tn
    in_specs = [pl.BlockSpec((tm, k), lambda j, i: (i, 0)),
                pl.BlockSpec((k, tn), lambda j, i: (0, j + off))]
    args = [a, w]
    kern = _mm_kernel
    if residual is not None:
        in_specs.append(pl.BlockSpec((tm, tn), lambda j, i: (i, j)))
        args.append(residual)
        kern = _mm_residual_kernel
    return pl.pallas_call(
        kern,
        grid=(n // tn, m // tm),
        in_specs=in_specs,
        out_specs=pl.BlockSpec((tm, tn), lambda j, i: (i, j)),
        out_shape=jax.ShapeDtypeStruct((m, n), out_dtype),
        scratch_shapes=[pltpu.VMEM((k, tn), BF16)],
        compiler_params=_params("parallel", "arbitrary"),
        name=name,
    )(*args)


def _mm_bf16_residual_kernel(a_ref, w_ref, r_ref, o_ref):
    o_ref[...] = r_ref[...] + _dot(a_ref[...], w_ref[...])


def _matmul_rows_resident(a, w, tm, tn, name, residual):
    m, k = a.shape
    n = w.shape[1]
    return pl.pallas_call(
        _mm_bf16_residual_kernel,
        grid=(m // tm, n // tn),
        in_specs=[pl.BlockSpec((tm, k), lambda i, j: (i, 0)),
                  pl.BlockSpec((k, tn), lambda i, j: (0, j)),
                  pl.BlockSpec((tm, tn), lambda i, j: (i, j))],
        out_specs=pl.BlockSpec((tm, tn), lambda i, j: (i, j)),
        out_shape=jax.ShapeDtypeStruct((m, n), F32),
        compiler_params=_params("parallel", "arbitrary"),
        name=name,
    )(a, w, residual)


def _glu_kernel(u_ref, wa_ref, wb_ref, o_ref, wab_ref, wbb_ref):
    _cast_weights_once([wa_ref, wb_ref], [wab_ref, wbb_ref])
    u = u_ref[...]
    o_ref[...] = _dot(u, wab_ref[...]) * _sigmoid(_dot(u, wbb_ref[...]))


def _glu(u, w_in, col_a, col_b, n, tm, tn):
    m, k = u.shape
    oa, ob = col_a // tn, col_b // tn
    return pl.pallas_call(
        _glu_kernel,
        grid=(n // tn, m // tm),
        in_specs=[pl.BlockSpec((tm, k), lambda j, i: (i, 0)),
                  pl.BlockSpec((k, tn), lambda j, i: (0, j + oa)),
                  pl.BlockSpec((k, tn), lambda j, i: (0, j + ob))],
        out_specs=pl.BlockSpec((tm, tn), lambda j, i: (i, j)),
        out_shape=jax.ShapeDtypeStruct((m, n), F32),
        scratch_shapes=[pltpu.VMEM((k, tn), BF16), pltpu.VMEM((k, tn), BF16)],
        compiler_params=_params("parallel", "arbitrary"),
        name="glu_proj",
    )(u, w_in, w_in)


def _moba_gate_kernel(q_ref, k_ref, o_ref, *, tq):
    s = k_ref.shape[0]
    nb = s // MOBA_BLOCK
    k = k_ref[...].astype(F32).reshape(nb, MOBA_BLOCK, HEAD_DIM)
    k_mean = jnp.sum(k, axis=1) * (1.0 / MOBA_BLOCK)
    q = q_ref[...]
    gate = jnp.zeros((nb, tq), F32)
    rest = k_mean
    for _ in range(3):
        part = rest.astype(BF16)
        gate = gate + lax.dot_general(part, q, (((1,), (1,)), ((), ())), preferred_element_type=F32)
        rest = rest - part.astype(F32)
    blk = lax.broadcasted_iota(jnp.int32, (nb, tq), 0)
    qpos = pl.program_id(1) * tq + lax.broadcasted_iota(jnp.int32, (nb, tq), 1)
    past = blk < qpos // MOBA_BLOCK
    gate = jnp.where(past, gate, NEG_INF)
    picked = jnp.zeros((nb, tq), jnp.bool_)
    for _ in range(min(MOBA_TOPK, nb)):
        top = jnp.max(gate, axis=0, keepdims=True)
        first = jnp.min(jnp.where(gate == top, blk, nb), axis=0, keepdims=True)
        hit = blk == first
        picked = jnp.logical_or(picked, hit)
        gate = jnp.where(hit, -jnp.inf, gate)
    o_ref[0] = jnp.where(jnp.logical_and(picked, past), 0.0, NEG_INF)


def _moba_gate(qkv, tq=2048):
    s = qkv.shape[0]
    nb = s // MOBA_BLOCK
    return pl.pallas_call(
        functools.partial(_moba_gate_kernel, tq=tq),
        grid=(ATTN_HEADS, s // tq),
        in_specs=[pl.BlockSpec((tq, HEAD_DIM), lambda h, i: (i, h)),
                  pl.BlockSpec((s, HEAD_DIM), lambda h, i: (0, ATTN_HEADS + h))],
        out_specs=pl.BlockSpec((1, nb, tq), lambda h, i: (h, 0, i)),
        out_shape=jax.ShapeDtypeStruct((ATTN_HEADS, nb, s), F32),
        compiler_params=_params("parallel", "arbitrary"),
        name="moba_gate",
    )(qkv, qkv)


SCORE_GROUP = 8
PROB_GROUP = 4
ONES_ROWS = 16
HEADS_PER_STEP = 2
LOG2E = 1.4426950408889634


def _moba_attn_kernel(zslope_ref, q_ref, k_ref, kb_ref, vt_ref, bias_ref, o_ref, s_ref):
    qb = pl.program_id(1)
    bs = MOBA_BLOCK
    hd = HEAD_DIM
    c = (hd ** -0.5) * LOG2E
    heads = range(HEADS_PER_STEP)
    zslope = [zslope_ref[pl.program_id(0) * HEADS_PER_STEP + a] for a in heads]
    nt = (((1,), (1,)), ((), ()))
    ones3 = (lax.broadcasted_iota(jnp.int32, (bs, hd), 1) < 3).astype(BF16)
    q2 = [jnp.concatenate([q_ref[:, a * hd:(a + 1) * hd], ones3], axis=1) for a in heads]

    def scores(a, row0, n_rows):
        rows = pl.ds(pl.multiple_of(row0, bs), n_rows)
        k2 = jnp.concatenate([k_ref[rows, a * hd:(a + 1) * hd], kb_ref[a, 0:n_rows, :]], axis=1)
        return lax.dot_general(k2, q2[a], nt, preferred_element_type=F32)

    def pick(a, kb):
        return zslope[a] * (kb - qb).astype(F32) + bias_ref[a, pl.ds(kb, 1), :]

    causal = lax.broadcasted_iota(jnp.int32, (bs, bs), 0) <= lax.broadcasted_iota(jnp.int32, (bs, bs), 1)
    s_own = [jnp.where(causal, scores(a, qb * bs, bs), NEG_INF) for a in heads]
    m = tuple(c * jnp.max(s_own[a], axis=0, keepdims=True) for a in heads)

    def pass1(g, m):
        m = list(m)
        for a in heads:
            s = scores(a, g * (SCORE_GROUP * bs), SCORE_GROUP * bs)
            s_ref[a, pl.ds(pl.multiple_of(g * (SCORE_GROUP * bs), bs), SCORE_GROUP * bs), :] = s
            for b in range(SCORE_GROUP):
                kb = g * SCORE_GROUP + b
                top = jnp.max(s[b * bs:(b + 1) * bs], axis=0, keepdims=True)
                m[a] = jnp.maximum(m[a], c * top + pick(a, kb))
        return tuple(m)

    m = lax.fori_loop(0, (qb + SCORE_GROUP - 1) // SCORE_GROUP, pass1, m)

    acc = tuple(_dot(vt_ref[a, qb], jnp.exp2(c * s_own[a] - m[a]).astype(BF16)) for a in heads)

    def pass2(g, acc):
        acc = list(acc)
        for b in range(PROB_GROUP):
            kb = g * PROB_GROUP + b
            for a in heads:
                s = s_ref[a, pl.ds(pl.multiple_of(kb * bs, bs), bs), :]
                w = pick(a, kb) - m[a]
                acc[a] = acc[a] + _dot(vt_ref[a, kb], jnp.exp2(c * s + w).astype(BF16))
        return tuple(acc)

    acc = lax.fori_loop(0, (qb + PROB_GROUP - 1) // PROB_GROUP, pass2, acc)
    for a in heads:
        o_ref[:, a * hd:(a + 1) * hd] = (acc[a][0:hd] / acc[a][hd:hd + 1]).T.astype(o_ref.dtype)


def _moba_attention(qkv, bias):
    s = qkv.shape[0]
    nb = s // MOBA_BLOCK
    assert nb % SCORE_GROUP == 0 and SCORE_GROUP % PROB_GROUP == 0
    scale = HEAD_DIM ** -0.5
    slopes = np.exp2(-8.0 * np.arange(1, ATTN_HEADS + 1, dtype=np.float32) / ATTN_HEADS).astype(np.float32)
    zslope = jnp.asarray(slopes * np.float32(LOG2E * MOBA_BLOCK))
    rest = (slopes[:, None] * np.arange(MOBA_BLOCK, dtype=np.float32)[None, :] / np.float32(scale)).astype(np.float32)
    cols = []
    for _ in range(3):
        part = rest.astype(BF16)
        cols.append(part)
        rest = rest - part.astype(np.float32)
    kbias = np.zeros((ATTN_HEADS, MOBA_BLOCK, HEAD_DIM), BF16)
    kbias[:, :, 0:3] = np.stack(cols, axis=-1)
    kbias = jnp.asarray(np.tile(kbias, (1, SCORE_GROUP, 1)))
    v = qkv[:, 2 * ATTN_HEADS * HEAD_DIM:].reshape(nb, MOBA_BLOCK, ATTN_HEADS, HEAD_DIM)
    vt = jnp.concatenate([v.transpose(2, 0, 3, 1),
                          jnp.ones((ATTN_HEADS, nb, ONES_ROWS, MOBA_BLOCK), BF16)], axis=2)
    hps = HEADS_PER_STEP
    groups = ATTN_HEADS // hps
    return pl.pallas_call(
        _moba_attn_kernel,
        grid=(groups, nb),
        in_specs=[pl.BlockSpec(memory_space=pltpu.SMEM),
                  pl.BlockSpec((MOBA_BLOCK, hps * HEAD_DIM), lambda h, i: (i, h)),
                  pl.BlockSpec((s, hps * HEAD_DIM), lambda h, i: (0, groups + h)),
                  pl.BlockSpec((hps, SCORE_GROUP * MOBA_BLOCK, HEAD_DIM), lambda h, i: (h, 0, 0)),
                  pl.BlockSpec((hps, nb, HEAD_DIM + ONES_ROWS, MOBA_BLOCK), lambda h, i: (h, 0, 0, 0)),
                  pl.BlockSpec((hps, nb, MOBA_BLOCK), lambda h, i: (h, 0, i))],
        out_specs=pl.BlockSpec((MOBA_BLOCK, hps * HEAD_DIM), lambda h, i: (i, h)),
        out_shape=jax.ShapeDtypeStruct((s, ATTN_HEADS * HEAD_DIM), BF16),
        scratch_shapes=[pltpu.VMEM((hps, s, MOBA_BLOCK), F32)],
        compiler_params=_params("parallel", "arbitrary"),
        name="moba_attention",
    )(zslope, qkv, qkv, kbias, vt, bias)


def _conv_ln_silu_kernel(x_ref, halo_ref, w_ref, b_ref, g_ref, beta_ref, o_ref, win_ref, y_ref, *, taps):
    i = pl.program_id(0)
    j = pl.program_id(1)
    nj = y_ref.shape[0]
    t, ct = x_ref.shape
    hr = halo_ref.shape[0]
    win_ref[0, 0:hr, :] = jnp.where(i == 0, 0.0, halo_ref[...])
    win_ref[0, hr:, :] = x_ref[...]
    span = hr + t - F32_SUBLANES
    for p in range(1, F32_SUBLANES):
        win_ref[p, 0:span, :] = win_ref[0, pl.ds(p, span), :]
    for c in range(ct // LANES):
        lanes = slice(c * LANES, (c + 1) * LANES)
        acc = jnp.broadcast_to(b_ref[:, lanes], (t, LANES))
        for k in range(taps):
            a, p = divmod(hr - (taps - 1) + k, F32_SUBLANES)
            acc = acc + w_ref[k:k + 1, lanes] * win_ref[p, F32_SUBLANES * a:F32_SUBLANES * a + t, lanes]
        y_ref[j, :, lanes] = acc

    @pl.when(j == nj - 1)
    def _():
        n = nj * ct
        mu = sum(jnp.sum(y_ref[jj], axis=-1, keepdims=True) for jj in range(nj)) * (1.0 / n)
        var = sum(jnp.sum(jnp.square(y_ref[jj] - mu), axis=-1, keepdims=True) for jj in range(nj)) * (1.0 / n)
        inv = lax.rsqrt(var + LN_EPS)
        for jj in range(nj):
            cols = slice(jj * ct, (jj + 1) * ct)
            y = (y_ref[jj] - mu) * inv * g_ref[:, cols] + beta_ref[:, cols]
            o_ref[:, cols] = (y * _sigmoid(y)).astype(o_ref.dtype)


def _conv_ln_silu(x, w, b, g, beta, rows=256, cols=512):
    s, c = x.shape
    taps = w.shape[0]
    assert taps - 1 <= CONV_HALO_ROWS
    per = rows // CONV_HALO_ROWS
    nj = c // cols
    return pl.pallas_call(
        functools.partial(_conv_ln_silu_kernel, taps=taps),
        grid=(s // rows, nj),
        in_specs=[pl.BlockSpec((rows, cols), lambda i, j: (i, j)),
                  pl.BlockSpec((CONV_HALO_ROWS, cols), lambda i, j: (jnp.maximum(i * per - 1, 0), j)),
                  pl.BlockSpec((taps, cols), lambda i, j: (0, j)),
                  pl.BlockSpec((1, cols), lambda i, j: (0, j)),
                  pl.BlockSpec((1, c), lambda i, j: (0, 0)),
                  pl.BlockSpec((1, c), lambda i, j: (0, 0))],
        out_specs=pl.BlockSpec((rows, c), lambda i, j: (i, 0)),
        out_shape=jax.ShapeDtypeStruct((s, c), BF16),
        scratch_shapes=[pltpu.VMEM((F32_SUBLANES, rows + CONV_HALO_ROWS, cols), F32),
                        pltpu.VMEM((nj, rows, cols), F32)],
        compiler_params=_params("parallel", "arbitrary"),
        name="conformer_conv_ln_silu",
    )(x, x, w, b.reshape(1, c), g.reshape(1, c), beta.reshape(1, c))


def _merge_kernel(u_ref, a_ref, c_ref, wga_ref, wgc_ref, wpa_ref, wpc_ref, o_ref, bga_ref, bgc_ref, bpa_ref, bpc_ref):
    _cast_weights_once([wga_ref, wgc_ref, wpa_ref, wpc_ref], [bga_ref, bgc_ref, bpa_ref, bpc_ref])
    u = u_ref[...]
    ga = _sigmoid(_dot(u, bga_ref[...]))
    gc = _sigmoid(_dot(u, bgc_ref[...]))
    o_ref[...] = (ga * _dot(a_ref[...], bpa_ref[...]) + gc * _dot(c_ref[...], bpc_ref[...])).astype(o_ref.dtype)


def _merge(u, attn, cact, w_in, col_ga, col_gc, w_pa, w_pc, tm, tn):
    m, d = u.shape
    n = w_pa.shape[1]
    oa, oc = col_ga // tn, col_gc // tn
    return pl.pallas_call(
        _merge_kernel,
        grid=(n // tn, m // tm),
        in_specs=[pl.BlockSpec((tm, d), lambda j, i: (i, 0)),
                  pl.BlockSpec((tm, attn.shape[1]), lambda j, i: (i, 0)),
                  pl.BlockSpec((tm, cact.shape[1]), lambda j, i: (i, 0)),
                  pl.BlockSpec((d, tn), lambda j, i: (0, j + oa)),
                  pl.BlockSpec((d, tn), lambda j, i: (0, j + oc)),
                  pl.BlockSpec((w_pa.shape[0], tn), lambda j, i: (0, j)),
                  pl.BlockSpec((w_pc.shape[0], tn), lambda j, i: (0, j))],
        out_specs=pl.BlockSpec((tm, tn), lambda j, i: (i, j)),
        out_shape=jax.ShapeDtypeStruct((m, n), BF16),
        scratch_shapes=[pltpu.VMEM((d, tn), BF16), pltpu.VMEM((d, tn), BF16),
                        pltpu.VMEM((w_pa.shape[0], tn), BF16), pltpu.VMEM((w_pc.shape[0], tn), BF16)],
        compiler_params=_params("parallel", "arbitrary"),
        name="gated_merge",
    )(u, attn, cact, w_in, w_in, w_pa, w_pc)


def _ffn_up_kernel(f_ref, fh_ref, wg_ref, wv_ref, cwg_ref, cwv_ref, cbg_ref, cbv_ref, o_ref, win_ref, bg_ref, bv_ref):
    _cast_weights_once([wg_ref, wv_ref], [bg_ref, bv_ref])
    tm = f_ref.shape[0]
    hr = fh_ref.shape[0]
    taps = cwg_ref.shape[0]
    keep = (pl.program_id(1) > 0).astype(F32)
    f = f_ref[...]
    fh = fh_ref[...]

    def conv(z, w_ref, cw_ref, cb_ref):
        up = _dot(f, w_ref[...])
        win_ref[z, 0:hr, :] = _dot(fh, w_ref[...]) * keep
        win_ref[z, hr:, :] = up
        y = cw_ref[taps - 1:taps, :] * up + cb_ref[...]
        for k in range(taps - 1):
            y = y + cw_ref[k:k + 1, :] * win_ref[z, pl.ds(hr - (taps - 1) + k, tm), :]
        return y

    g = conv(0, bg_ref, cwg_ref, cbg_ref)
    act = g * _sigmoid(g)
    v = conv(1, bv_ref, cwv_ref, cbv_ref)
    o_ref[...] = (act * v).astype(o_ref.dtype)


def _ffn_up(f, w_up, w_dw, b_dw, d_ff, tm, tn):
    m, d = f.shape
    taps = w_dw.shape[0]
    off = d_ff // tn
    per = tm // BF16_SUBLANES
    return pl.pallas_call(
        _ffn_up_kernel,
        grid=(d_ff // tn, m // tm),
        in_specs=[pl.BlockSpec((tm, d), lambda j, i: (i, 0)),
                  pl.BlockSpec((BF16_SUBLANES, d), lambda j, i: (jnp.maximum(i * per - 1, 0), 0)),
                  pl.BlockSpec((d, tn), lambda j, i: (0, j)),
                  pl.BlockSpec((d, tn), lambda j, i: (0, j + off)),
                  pl.BlockSpec((taps, tn), lambda j, i: (0, j)),
                  pl.BlockSpec((taps, tn), lambda j, i: (0, j + off)),
                  pl.BlockSpec((1, tn), lambda j, i: (0, j)),
                  pl.BlockSpec((1, tn), lambda j, i: (0, j + off))],
        out_specs=pl.BlockSpec((tm, tn), lambda j, i: (i, j)),
        out_shape=jax.ShapeDtypeStruct((m, d_ff), BF16),
        scratch_shapes=[pltpu.VMEM((2, tm + BF16_SUBLANES, tn), F32),
                        pltpu.VMEM((d, tn), BF16), pltpu.VMEM((d, tn), BF16)],
        compiler_params=_params("parallel", "arbitrary"),
        name="convffn_up",
    )(f, f, w_up, w_up, w_dw, w_dw, b_dw.reshape(1, -1), b_dw.reshape(1, -1))


def kernel(x, g_mix, w_in, w_conv_dw, b_conv_dw, ln_conv_g, ln_conv_b, w_proj_attn, w_proj_conv, w_out,
           g_ffn, w_up, w_ffn_dw, b_ffn_dw, w_down, g_final):
    b, s, d = x.shape
    assert b == 1, "single-sequence layer"
    attn_w = ATTN_HEADS * HEAD_DIM
    conv_c = w_conv_dw.shape[1]
    d_ff = w_down.shape[0]
    col_glu_a = 3 * attn_w
    col_glu_b = col_glu_a + conv_c
    col_ga = col_glu_b + conv_c
    col_gc = col_ga + d

    x2 = x.reshape(s, d)
    u = _rmsnorm(x2, g_mix, BF16, "rmsnorm_mix")
    qkv = _matmul(u, w_in, 3 * attn_w, 0, 1024, 512, BF16, "qkv_proj")
    c0 = _glu(u, w_in, col_glu_a, col_glu_b, conv_c, 1024, 256)

    bias = _moba_gate(qkv)
    attn = _moba_attention(qkv, bias)

    cact = _conv_ln_silu(c0, w_conv_dw, b_conv_dw, ln_conv_g, ln_conv_b)

    merged = _merge(u, attn, cact, w_in, col_ga, col_gc, w_proj_attn, w_proj_conv, 512, 256)
    h = _matmul(merged, w_out, d, 0, 1024, 512, F32, "out_proj", residual=x2)

    f = _rmsnorm(h, g_ffn, BF16, "rmsnorm_ffn")
    g = _ffn_up(f, w_up, w_ffn_dw, b_ffn_dw, d_ff, 1024, 256)
    h2 = _matmul_rows_resident(g, w_down.astype(BF16), 512, 512, "ffn_down", residual=h)
    out = _rmsnorm(h2, g_final, F32, "rmsnorm_final")
    return out.reshape(b, s, d)
```

```python
import functools

import jax
import jax.numpy as jnp
import numpy as np
from jax import lax
from jax.experimental import pallas as pl
from jax.experimental.pallas import tpu as pltpu

ATTN_HEADS = 16
HEAD_DIM = 128
MOBA_BLOCK = 256
MOBA_TOPK = 3
NORM_EPS = 1e-6
LN_EPS = 1e-5
NEG_INF = -1e30

F32 = jnp.float32
BF16 = jnp.bfloat16

V7X_VMEM_BYTES = 64 * 1024 * 1024
VMEM_LIMIT_BYTES = V7X_VMEM_BYTES - 8 * 1024 * 1024
LANES = 128
F32_SUBLANES = 8
BF16_SUBLANES = 16
CONV_HALO_ROWS = 32


def _params(*sem):
    return pltpu.CompilerParams(dimension_semantics=sem, vmem_limit_bytes=VMEM_LIMIT_BYTES)


def _dot(a, b):
    return jnp.dot(a, b, preferred_element_type=F32)


def _sigmoid(x):
    return 1.0 / (1.0 + jnp.exp(-x))


def _cast_weights_once(w_refs, wb_refs):
    @pl.when(pl.program_id(1) == 0)
    def _():
        for w_ref, wb_ref in zip(w_refs, wb_refs):
            wb_ref[...] = w_ref[...].astype(BF16)


def _rmsnorm_kernel(x_ref, g_ref, o_ref):
    x = x_ref[...]
    ms = jnp.mean(x * x, axis=-1, keepdims=True)
    o_ref[...] = (x * lax.rsqrt(ms + NORM_EPS) * g_ref[...]).astype(o_ref.dtype)


def _rmsnorm(x, g, out_dtype, name, rows=256):
    s, d = x.shape
    return pl.pallas_call(
        _rmsnorm_kernel,
        grid=(s // rows,),
        in_specs=[pl.BlockSpec((rows, d), lambda i: (i, 0)),
                  pl.BlockSpec((1, d), lambda i: (0, 0))],
        out_specs=pl.BlockSpec((rows, d), lambda i: (i, 0)),
        out_shape=jax.ShapeDtypeStruct((s, d), out_dtype),
        compiler_params=_params("parallel"),
        name=name,
    )(x, g.reshape(1, d))


def _mm_kernel(a_ref, w_ref, o_ref, wb_ref):
    _cast_weights_once([w_ref], [wb_ref])
    o_ref[...] = _dot(a_ref[...], wb_ref[...]).astype(o_ref.dtype)


def _mm_residual_kernel(a_ref, w_ref, r_ref, o_ref, wb_ref):
    _cast_weights_once([w_ref], [wb_ref])
    o_ref[...] = r_ref[...] + _dot(a_ref[...], wb_ref[...])


def _matmul(a, w, n, col0, tm, tn, out_dtype, name, residual=None):
    m, k = a.shape
    off = col0 // tn
    in_specs = [pl.BlockSpec((tm, k), lambda j, i: (i, 0)),
                pl.BlockSpec((k, tn), lambda j, i: (0, j + off))]
    args = [a, w]
    kern = _mm_kernel
    if residual is not None:
        in_specs.append(pl.BlockSpec((tm, tn), lambda j, i: (i, j)))
        args.append(residual)
        kern = _mm_residual_kernel
    return pl.pallas_call(
        kern,
        grid=(n // tn, m // tm),
        in_specs=in_specs,
        out_specs=pl.BlockSpec((tm, tn), lambda j, i: (i, j)),
        out_shape=jax.ShapeDtypeStruct((m, n), out_dtype),
        scratch_shapes=[pltpu.VMEM((k, tn), BF16)],
        compiler_params=_params("parallel", "arbitrary"),
        name=name,
    )(*args)


def _mm_bf16_residual_kernel(a_ref, w_ref, r_ref, o_ref):
    o_ref[...] = r_ref[...] + _dot(a_ref[...], w_ref[...])


def _matmul_rows_resident(a, w, tm, tn, name, residual):
    m, k = a.shape
    n = w.shape[1]
    return pl.pallas_call(
        _mm_bf16_residual_kernel,
        grid=(m // tm, n // tn),
        in_specs=[pl.BlockSpec((tm, k), lambda i, j: (i, 0)),
                  pl.BlockSpec((k, tn), lambda i, j: (0, j)),
                  pl.BlockSpec((tm, tn), lambda i, j: (i, j))],
        out_specs=pl.BlockSpec((tm, tn), lambda i, j: (i, j)),
        out_shape=jax.ShapeDtypeStruct((m, n), F32),
        compiler_params=_params("parallel", "arbitrary"),
        name=name,
    )(a, w, residual)


def _glu_kernel(u_ref, wa_ref, wb_ref, o_ref, wab_ref, wbb_ref):
    _cast_weights_once([wa_ref, wb_ref], [wab_ref, wbb_ref])
    u = u_ref[...]
    o_ref[...] = _dot(u, wab_ref[...]) * _sigmoid(_dot(u, wbb_ref[...]))


def _glu(u, w_in, col_a, col_b, n, tm, tn):
    m, k = u.shape
    oa, ob = col_a // tn, col_b // tn
    return pl.pallas_call(
        _glu_kernel,
        grid=(n // tn, m // tm),
        in_specs=[pl.BlockSpec((tm, k), lambda j, i: (i, 0)),
                  pl.BlockSpec((k, tn), lambda j, i: (0, j + oa)),
                  pl.BlockSpec((k, tn), lambda j, i: (0, j + ob))],
        out_specs=pl.BlockSpec((tm, tn), lambda j, i: (i, j)),
        out_shape=jax.ShapeDtypeStruct((m, n), F32),
        scratch_shapes=[pltpu.VMEM((k, tn), BF16), pltpu.VMEM((k, tn), BF16)],
        compiler_params=_params("parallel", "arbitrary"),
        name="glu_proj",
    )(u, w_in, w_in)


def _moba_gate_kernel(q_ref, k_ref, o_ref, *, tq):
    s = k_ref.shape[0]
    nb = s // MOBA_BLOCK
    k = k_ref[...].astype(F32).reshape(nb, MOBA_BLOCK, HEAD_DIM)
    k_mean = jnp.sum(k, axis=1) * (1.0 / MOBA_BLOCK)
    q = q_ref[...]
    gate = jnp.zeros((nb, tq), F32)
    rest = k_mean
    for _ in range(3):
        part = rest.astype(BF16)
        gate = gate + lax.dot_general(part, q, (((1,), (1,)), ((), ())), preferred_element_type=F32)
        rest = rest - part.astype(F32)
    blk = lax.broadcasted_iota(jnp.int32, (nb, tq), 0)
    qpos = pl.program_id(1) * tq + lax.broadcasted_iota(jnp.int32, (nb, tq), 1)
    past = blk < qpos // MOBA_BLOCK
    gate = jnp.where(past, gate, NEG_INF)
    picked = jnp.zeros((nb, tq), jnp.bool_)
    for _ in range(min(MOBA_TOPK, nb)):
        top = jnp.max(gate, axis=0, keepdims=True)
        first = jnp.min(jnp.where(gate == top, blk, nb), axis=0, keepdims=True)
        hit = blk == first
        picked = jnp.logical_or(picked, hit)
        gate = jnp.where(hit, -jnp.inf, gate)
    o_ref[0] = jnp.where(jnp.logical_and(picked, past), 0.0, NEG_INF)


def _moba_gate(qkv, tq=2048):
    s = qkv.shape[0]
    nb = s // MOBA_BLOCK
    return pl.pallas_call(
        functools.partial(_moba_gate_kernel, tq=tq),
        grid=(ATTN_HEADS, s // tq),
        in_specs=[pl.BlockSpec((tq, HEAD_DIM), lambda h, i: (i, h)),
                  pl.BlockSpec((s, HEAD_DIM), lambda h, i: (0, ATTN_HEADS + h))],
        out_specs=pl.BlockSpec((1, nb, tq), lambda h, i: (h, 0, i)),
        out_shape=jax.ShapeDtypeStruct((ATTN_HEADS, nb, s), F32),
        compiler_params=_params("parallel", "arbitrary"),
        name="moba_gate",
    )(qkv, qkv)


SCORE_GROUP = 8
PROB_GROUP = 4
ONES_ROWS = 16
HEADS_PER_STEP = 4
LOG2E = 1.4426950408889634


def _moba_attn_kernel(zslope_ref, q_ref, k_ref, kb_ref, vt_ref, bias_ref, o_ref, s_ref):
    qb = pl.program_id(1)
    bs = MOBA_BLOCK
    hd = HEAD_DIM
    c = (hd ** -0.5) * LOG2E
    heads = range(HEADS_PER_STEP)
    zslope = [zslope_ref[pl.program_id(0) * HEADS_PER_STEP + a] for a in heads]
    nt = (((1,), (1,)), ((), ()))
    ones3 = (lax.broadcasted_iota(jnp.int32, (bs, hd), 1) < 3).astype(BF16)
    q2 = [jnp.concatenate([q_ref[:, a * hd:(a + 1) * hd], ones3], axis=1) for a in heads]

    def scores(a, row0, n_rows):
        rows = pl.ds(pl.multiple_of(row0, bs), n_rows)
        k2 = jnp.concatenate([k_ref[rows, a * hd:(a + 1) * hd], kb_ref[a, 0:n_rows, :]], axis=1)
        return lax.dot_general(k2, q2[a], nt, preferred_element_type=F32)

    def pick(a, kb):
        return zslope[a] * (kb - qb).astype(F32) + bias_ref[a, pl.ds(kb, 1), :]

    causal = lax.broadcasted_iota(jnp.int32, (bs, bs), 0) <= lax.broadcasted_iota(jnp.int32, (bs, bs), 1)
    s_own = [jnp.where(causal, scores(a, qb * bs, bs), NEG_INF) for a in heads]
    m = tuple(c * jnp.max(s_own[a], axis=0, keepdims=True) for a in heads)

    def pass1(g, m):
        m = list(m)
        for a in heads:
            s = scores(a, g * (SCORE_GROUP * bs), SCORE_GROUP * bs)
            s_ref[a, pl.ds(pl.multiple_of(g * (SCORE_GROUP * bs), bs), SCORE_GROUP * bs), :] = s
            for b in range(SCORE_GROUP):
                kb = g * SCORE_GROUP + b
                top = jnp.max(s[b * bs:(b + 1) * bs], axis=0, keepdims=True)
                m[a] = jnp.maximum(m[a], c * top + pick(a, kb))
        return tuple(m)

    m = lax.fori_loop(0, (qb + SCORE_GROUP - 1) // SCORE_GROUP, pass1, m)

    acc = tuple(_dot(vt_ref[a, qb], jnp.exp2(c * s_own[a] - m[a]).astype(BF16)) for a in heads)

    def pass2(g, acc):
        acc = list(acc)
        for b in range(PROB_GROUP):
            kb = g * PROB_GROUP + b
            for a in heads:
                s = s_ref[a, pl.ds(pl.multiple_of(kb * bs, bs), bs), :]
                w = pick(a, kb) - m[a]
                acc[a] = acc[a] + _dot(vt_ref[a, kb], jnp.exp2(c * s + w).astype(BF16))
        return tuple(acc)

    acc = lax.fori_loop(0, (qb + PROB_GROUP - 1) // PROB_GROUP, pass2, acc)
    for a in heads:
        o_ref[:, a * hd:(a + 1) * hd] = (acc[a][0:hd] / acc[a][hd:hd + 1]).T.astype(o_ref.dtype)


def _moba_attention(qkv, bias):
    s = qkv.shape[0]
    nb = s // MOBA_BLOCK
    assert nb % SCORE_GROUP == 0 and SCORE_GROUP % PROB_GROUP == 0
    scale = HEAD_DIM ** -0.5
    slopes = np.exp2(-8.0 * np.arange(1, ATTN_HEADS + 1, dtype=np.float32) / ATTN_HEADS).astype(np.float32)
    zslope = jnp.asarray(slopes * np.float32(LOG2E * MOBA_BLOCK))
    rest = (slopes[:, None] * np.arange(MOBA_BLOCK, dtype=np.float32)[None, :] / np.float32(scale)).astype(np.float32)
    cols = []
    for _ in range(3):
        part = rest.astype(BF16)
        cols.append(part)
        rest = rest - part.astype(np.float32)
    kbias = np.zeros((ATTN_HEADS, MOBA_BLOCK, HEAD_DIM), BF16)
    kbias[:, :, 0:3] = np.stack(cols, axis=-1)
    kbias = jnp.asarray(np.tile(kbias, (1, SCORE_GROUP, 1)))
    v = qkv[:, 2 * ATTN_HEADS * HEAD_DIM:].reshape(nb, MOBA_BLOCK, ATTN_HEADS, HEAD_DIM)
    vt = jnp.concatenate([v.transpose(2, 0, 3, 1),
                          jnp.ones((ATTN_HEADS, nb, ONES_ROWS, MOBA_BLOCK), BF16)], axis=2)
    hps = HEADS_PER_STEP
    groups = ATTN_HEADS // hps
    once = pl.Buffered(1)
    return pl.pallas_call(
        _moba_attn_kernel,
        grid=(groups, nb),
        in_specs=[pl.BlockSpec(memory_space=pltpu.SMEM),
                  pl.BlockSpec((MOBA_BLOCK, hps * HEAD_DIM), lambda h, i: (i, h)),
                  pl.BlockSpec((s, hps * HEAD_DIM), lambda h, i: (0, groups + h), pipeline_mode=once),
                  pl.BlockSpec((hps, SCORE_GROUP * MOBA_BLOCK, HEAD_DIM), lambda h, i: (h, 0, 0), pipeline_mode=once),
                  pl.BlockSpec((hps, nb, HEAD_DIM + ONES_ROWS, MOBA_BLOCK), lambda h, i: (h, 0, 0, 0),
                               pipeline_mode=once),
                  pl.BlockSpec((hps, nb, MOBA_BLOCK), lambda h, i: (h, 0, i))],
        out_specs=pl.BlockSpec((MOBA_BLOCK, hps * HEAD_DIM), lambda h, i: (i, h)),
        out_shape=jax.ShapeDtypeStruct((s, ATTN_HEADS * HEAD_DIM), BF16),
        scratch_shapes=[pltpu.VMEM((hps, s, MOBA_BLOCK), F32)],
        compiler_params=_params("parallel", "arbitrary"),
        name="moba_attention",
    )(zslope, qkv, qkv, kbias, vt, bias)


def _conv_ln_silu_kernel(x_ref, halo_ref, w_ref, b_ref, g_ref, beta_ref, o_ref, win_ref, y_ref, *, taps):
    i = pl.program_id(0)
    j = pl.program_id(1)
    nj = y_ref.shape[0]
    t, ct = x_ref.shape
    hr = halo_ref.shape[0]
    win_ref[0, 0:hr, :] = jnp.where(i == 0, 0.0, halo_ref[...])
    win_ref[0, hr:, :] = x_ref[...]
    span = hr + t - F32_SUBLANES
    for p in range(1, F32_SUBLANES):
        win_ref[p, 0:span, :] = win_ref[0, pl.ds(p, span), :]
    for c in range(ct // LANES):
        lanes = slice(c * LANES, (c + 1) * LANES)
        acc = jnp.broadcast_to(b_ref[:, lanes], (t, LANES))
        for k in range(taps):
            a, p = divmod(hr - (taps - 1) + k, F32_SUBLANES)
            acc = acc + w_ref[k:k + 1, lanes] * win_ref[p, F32_SUBLANES * a:F32_SUBLANES * a + t, lanes]
        y_ref[j, :, lanes] = acc

    @pl.when(j == nj - 1)
    def _():
        n = nj * ct
        mu = sum(jnp.sum(y_ref[jj], axis=-1, keepdims=True) for jj in range(nj)) * (1.0 / n)
        var = sum(jnp.sum(jnp.square(y_ref[jj] - mu), axis=-1, keepdims=True) for jj in range(nj)) * (1.0 / n)
        inv = lax.rsqrt(var + LN_EPS)
        for jj in range(nj):
            cols = slice(jj * ct, (jj + 1) * ct)
            y = (y_ref[jj] - mu) * inv * g_ref[:, cols] + beta_ref[:, cols]
            o_ref[:, cols] = (y * _sigmoid(y)).astype(o_ref.dtype)


def _conv_ln_silu(x, w, b, g, beta, rows=256, cols=512):
    s, c = x.shape
    taps = w.shape[0]
    assert taps - 1 <= CONV_HALO_ROWS
    per = rows // CONV_HALO_ROWS
    nj = c // cols
    return pl.pallas_call(
        functools.partial(_conv_ln_silu_kernel, taps=taps),
        grid=(s // rows, nj),
        in_specs=[pl.BlockSpec((rows, cols), lambda i, j: (i, j)),
                  pl.BlockSpec((CONV_HALO_ROWS, cols), lambda i, j: (jnp.maximum(i * per - 1, 0), j)),
                  pl.BlockSpec((taps, cols), lambda i, j: (0, j)),
                  pl.BlockSpec((1, cols), lambda i, j: (0, j)),
                  pl.BlockSpec((1, c), lambda i, j: (0, 0)),
                  pl.BlockSpec((1, c), lambda i, j: (0, 0))],
        out_specs=pl.BlockSpec((rows, c), lambda i, j: (i, 0)),
        out_shape=jax.ShapeDtypeStruct((s, c), BF16),
        scratch_shapes=[pltpu.VMEM((F32_SUBLANES, rows + CONV_HALO_ROWS, cols), F32),
                        pltpu.VMEM((nj, rows, cols), F32)],
        compiler_params=_params("parallel", "arbitrary"),
        name="conformer_conv_ln_silu",
    )(x, x, w, b.reshape(1, c), g.reshape(1, c), beta.reshape(1, c))


def _merge_kernel(u_ref, a_ref, c_ref, wga_ref, wgc_ref, wpa_ref, wpc_ref, o_ref, bga_ref, bgc_ref, bpa_ref, bpc_ref):
    _cast_weights_once([wga_ref, wgc_ref, wpa_ref, wpc_ref], [bga_ref, bgc_ref, bpa_ref, bpc_ref])
    u = u_ref[...]
    ga = _sigmoid(_dot(u, bga_ref[...]))
    gc = _sigmoid(_dot(u, bgc_ref[...]))
    o_ref[...] = (ga * _dot(a_ref[...], bpa_ref[...]) + gc * _dot(c_ref[...], bpc_ref[...])).astype(o_ref.dtype)


def _merge(u, attn, cact, w_in, col_ga, col_gc, w_pa, w_pc, tm, tn):
    m, d = u.shape
    n = w_pa.shape[1]
    oa, oc = col_ga // tn, col_gc // tn
    return pl.pallas_call(
        _merge_kernel,
        grid=(n // tn, m // tm),
        in_specs=[pl.BlockSpec((tm, d), lambda j, i: (i, 0)),
                  pl.BlockSpec((tm, attn.shape[1]), lambda j, i: (i, 0)),
                  pl.BlockSpec((tm, cact.shape[1]), lambda j, i: (i, 0)),
                  pl.BlockSpec((d, tn), lambda j, i: (0, j + oa)),
                  pl.BlockSpec((d, tn), lambda j, i: (0, j + oc)),
                  pl.BlockSpec((w_pa.shape[0], tn), lambda j, i: (0, j)),
                  pl.BlockSpec((w_pc.shape[0], tn), lambda j, i: (0, j))],
        out_specs=pl.BlockSpec((tm, tn), lambda j, i: (i, j)),
        out_shape=jax.ShapeDtypeStruct((m, n), BF16),
        scratch_shapes=[pltpu.VMEM((d, tn), BF16), pltpu.VMEM((d, tn), BF16),
                        pltpu.VMEM((w_pa.shape[0], tn), BF16), pltpu.VMEM((w_pc.shape[0], tn), BF16)],
        compiler_params=_params("parallel", "arbitrary"),
        name="gated_merge",
    )(u, attn, cact, w_in, w_in, w_pa, w_pc)


def _ffn_up_kernel(f_ref, fh_ref, wg_ref, wv_ref, cwg_ref, cwv_ref, cbg_ref, cbv_ref, o_ref, win_ref, bg_ref, bv_ref):
    _cast_weights_once([wg_ref, wv_ref], [bg_ref, bv_ref])
    tm = f_ref.shape[0]
    hr = fh_ref.shape[0]
    taps = cwg_ref.shape[0]
    keep = (pl.program_id(1) > 0).astype(F32)
    f = f_ref[...]
    fh = fh_ref[...]

    def conv(z, w_ref, cw_ref, cb_ref):
        up = _dot(f, w_ref[...])
        win_ref[z, 0:hr, :] = _dot(fh, w_ref[...]) * keep
        win_ref[z, hr:, :] = up
        y = cw_ref[taps - 1:taps, :] * up + cb_ref[...]
        for k in range(taps - 1):
            y = y + cw_ref[k:k + 1, :] * win_ref[z, pl.ds(hr - (taps - 1) + k, tm), :]
        return y

    g = conv(0, bg_ref, cwg_ref, cbg_ref)
    act = g * _sigmoid(g)
    v = conv(1, bv_ref, cwv_ref, cbv_ref)
    o_ref[...] = (act * v).astype(o_ref.dtype)


def _ffn_up(f, w_up, w_dw, b_dw, d_ff, tm, tn):
    m, d = f.shape
    taps = w_dw.shape[0]
    off = d_ff // tn
    per = tm // BF16_SUBLANES
    return pl.pallas_call(
        _ffn_up_kernel,
        grid=(d_ff // tn, m // tm),
        in_specs=[pl.BlockSpec((tm, d), lambda j, i: (i, 0)),
                  pl.BlockSpec((BF16_SUBLANES, d), lambda j, i: (jnp.maximum(i * per - 1, 0), 0)),
                  pl.BlockSpec((d, tn), lambda j, i: (0, j)),
                  pl.BlockSpec((d, tn), lambda j, i: (0, j + off)),
                  pl.BlockSpec((taps, tn), lambda j, i: (0, j)),
                  pl.BlockSpec((taps, tn), lambda j, i: (0, j + off)),
                  pl.BlockSpec((1, tn), lambda j, i: (0, j)),
                  pl.BlockSpec((1, tn), lambda j, i: (0, j + off))],
        out_specs=pl.BlockSpec((tm, tn), lambda j, i: (i, j)),
        out_shape=jax.ShapeDtypeStruct((m, d_ff), BF16),
        scratch_shapes=[pltpu.VMEM((2, tm + BF16_SUBLANES, tn), F32),
                        pltpu.VMEM((d, tn), BF16), pltpu.VMEM((d, tn), BF16)],
        compiler_params=_params("parallel", "arbitrary"),
        name="convffn_up",
    )(f, f, w_up, w_up, w_dw, w_dw, b_dw.reshape(1, -1), b_dw.reshape(1, -1))


def kernel(x, g_mix, w_in, w_conv_dw, b_conv_dw, ln_conv_g, ln_conv_b, w_proj_attn, w_proj_conv, w_out,
           g_ffn, w_up, w_ffn_dw, b_ffn_dw, w_down, g_final):
    b, s, d = x.shape
    assert b == 1, "single-sequence layer"
    attn_w = ATTN_HEADS * HEAD_DIM
    conv_c = w_conv_dw.shape[1]
    d_ff = w_down.shape[0]
    col_glu_a = 3 * attn_w
    col_glu_b = col_glu_a + conv_c
    col_ga = col_glu_b + conv_c
    col_gc = col_ga + d

    x2 = x.reshape(s, d)
    u = _rmsnorm(x2, g_mix, BF16, "rmsnorm_mix")
    qkv = _matmul(u, w_in, 3 * attn_w, 0, 1024, 512, BF16, "qkv_proj")
    c0 = _glu(u, w_in, col_glu_a, col_glu_b, conv_c, 1024, 256)

    bias = _moba_gate(qkv)
    attn = _moba_attention(qkv, bias)

    cact = _conv_ln_silu(c0, w_conv_dw, b_conv_dw, ln_conv_g, ln_conv_b)

    merged = _merge(u, attn, cact, w_in, col_ga, col_gc, w_proj_attn, w_proj_conv, 512, 256)
    h = _matmul(merged, w_out, d, 0, 1024, 512, F32, "out_proj", residual=x2)

    f = _rmsnorm(h, g_ffn, BF16, "rmsnorm_ffn")
    g = _ffn_up(f, w_up, w_ffn_dw, b_ffn_dw, d_ff, 1024, 256)
    h2 = _matmul_rows_resident(g, w_down.astype(BF16), 512, 512, "ffn_down", residual=h)
    out = _rmsnorm(h2, g_final, F32, "rmsnorm_final")
    return out.reshape(b, s, d)
```

```python
import functools

import jax
import jax.numpy as jnp
import numpy as np
from jax import lax
from jax.experimental import pallas as pl
from jax.experimental.pallas import tpu as pltpu

ATTN_HEADS = 16
HEAD_DIM = 128
MOBA_BLOCK = 256
MOBA_TOPK = 3
NORM_EPS = 1e-6
LN_EPS = 1e-5
NEG_INF = -1e30

F32 = jnp.float32
BF16 = jnp.bfloat16

V7X_VMEM_BYTES = 64 * 1024 * 1024
VMEM_LIMIT_BYTES = V7X_VMEM_BYTES - 8 * 1024 * 1024
LANES = 128
F32_SUBLANES = 8
BF16_SUBLANES = 16
CONV_HALO_ROWS = 32
ONES_ROWS = 16


def _params(*sem):
    return pltpu.CompilerParams(dimension_semantics=sem, vmem_limit_bytes=VMEM_LIMIT_BYTES)


def _dot(a, b):
    return jnp.dot(a, b, preferred_element_type=F32)


def _sigmoid(x):
    return 1.0 / (1.0 + jnp.exp(-x))


def _cast_weights_once(w_refs, wb_refs):
    @pl.when(pl.program_id(1) == 0)
    def _():
        for w_ref, wb_ref in zip(w_refs, wb_refs):
            wb_ref[...] = w_ref[...].astype(BF16)


def _rmsnorm_kernel(x_ref, g_ref, o_ref):
    x = x_ref[...]
    ms = jnp.mean(x * x, axis=-1, keepdims=True)
    o_ref[...] = (x * lax.rsqrt(ms + NORM_EPS) * g_ref[...]).astype(o_ref.dtype)


def _rmsnorm(x, g, out_dtype, name, rows=512):
    s, d = x.shape
    return pl.pallas_call(
        _rmsnorm_kernel,
        grid=(s // rows,),
        in_specs=[pl.BlockSpec((rows, d), lambda i: (i, 0)),
                  pl.BlockSpec((1, d), lambda i: (0, 0))],
        out_specs=pl.BlockSpec((rows, d), lambda i: (i, 0)),
        out_shape=jax.ShapeDtypeStruct((s, d), out_dtype),
        compiler_params=_params("parallel"),
        name=name,
    )(x, g.reshape(1, d))


def _mm_kernel(a_ref, w_ref, o_ref, wb_ref):
    _cast_weights_once([w_ref], [wb_ref])
    o_ref[...] = _dot(a_ref[...], wb_ref[...]).astype(o_ref.dtype)


def _mm_residual_kernel(a_ref, w_ref, r_ref, o_ref, wb_ref):
    _cast_weights_once([w_ref], [wb_ref])
    o_ref[...] = r_ref[...] + _dot(a_ref[...], wb_ref[...])


def _matmul(a, w, n, col0, tm, tn, out_dtype, name, residual=None):
    m, k = a.shape
    off = col0 // tn
    in_specs = [pl.BlockSpec((tm, k), lambda j, i: (i, 0)),
                pl.BlockSpec((k, tn), lambda j, i: (0, j + off))]
    args = [a, w]
    kern = _mm_kernel
    if residual is not None:
        in_specs.append(pl.BlockSpec((tm, tn), lambda j, i: (i, j)))
        args.append(residual)
        kern = _mm_residual_kernel
    return pl.pallas_call(
        kern,
        grid=(n // tn, m // tm),
        in_specs=in_specs,
        out_specs=pl.BlockSpec((tm, tn), lambda j, i: (i, j)),
        out_shape=jax.ShapeDtypeStruct((m, n), out_dtype),
        scratch_shapes=[pltpu.VMEM((k, tn), BF16)],
        compiler_params=_params("parallel", "arbitrary"),
        name=name,
    )(*args)


def _v_proj_kernel(a_ref, w_ref, o_ref, wb_ref):
    _cast_weights_once([w_ref], [wb_ref])
    acc = _dot(a_ref[...], wb_ref[...])
    n_heads, n_blocks = o_ref.shape[0], o_ref.shape[1]
    ones = jnp.ones((ONES_ROWS, MOBA_BLOCK), o_ref.dtype)
    for hh in range(n_heads):
        for rb in range(n_blocks):
            blk = acc[rb * MOBA_BLOCK:(rb + 1) * MOBA_BLOCK, hh * HEAD_DIM:(hh + 1) * HEAD_DIM]
            o_ref[hh, rb, 0:HEAD_DIM, :] = blk.T.astype(o_ref.dtype)
            o_ref[hh, rb, HEAD_DIM:, :] = ones


def _v_proj(a, w, col0, tm, tn):
    m, k = a.shape
    off = col0 // tn
    nb = m // MOBA_BLOCK
    return pl.pallas_call(
        _v_proj_kernel,
        grid=(ATTN_HEADS * HEAD_DIM // tn, m // tm),
        in_specs=[pl.BlockSpec((tm, k), lambda j, i: (i, 0)),
                  pl.BlockSpec((k, tn), lambda j, i: (0, j + off))],
        out_specs=pl.BlockSpec((tn // HEAD_DIM, tm // MOBA_BLOCK, HEAD_DIM + ONES_ROWS, MOBA_BLOCK),
                               lambda j, i: (j, i, 0, 0)),
        out_shape=jax.ShapeDtypeStruct((ATTN_HEADS, nb, HEAD_DIM + ONES_ROWS, MOBA_BLOCK), BF16),
        scratch_shapes=[pltpu.VMEM((k, tn), BF16)],
        compiler_params=_params("parallel", "arbitrary"),
        name="v_proj",
    )(a, w)


def _mm_bf16_residual_kernel(a_ref, w_ref, r_ref, o_ref):
    o_ref[...] = r_ref[...] + _dot(a_ref[...], w_ref[...])


def _matmul_rows_resident(a, w, tm, tn, name, residual):
    m, k = a.shape
    n = w.shape[1]
    return pl.pallas_call(
        _mm_bf16_residual_kernel,
        grid=(m // tm, n // tn),
        in_specs=[pl.BlockSpec((tm, k), lambda i, j: (i, 0)),
                  pl.BlockSpec((k, tn), lambda i, j: (0, j)),
                  pl.BlockSpec((tm, tn), lambda i, j: (i, j))],
        out_specs=pl.BlockSpec((tm, tn), lambda i, j: (i, j)),
        out_shape=jax.ShapeDtypeStruct((m, n), F32),
        compiler_params=_params("parallel", "arbitrary"),
        name=name,
    )(a, w, residual)


def _glu_kernel(u_ref, wa_ref, wb_ref, o_ref, wab_ref, wbb_ref):
    _cast_weights_once([wa_ref, wb_ref], [wab_ref, wbb_ref])
    u = u_ref[...]
    o_ref[...] = _dot(u, wab_ref[...]) * _sigmoid(_dot(u, wbb_ref[...]))


def _glu(u, w_in, col_a, col_b, n, tm, tn):
    m, k = u.shape
    oa, ob = col_a // tn, col_b // tn
    return pl.pallas_call(
        _glu_kernel,
        grid=(n // tn, m // tm),
        in_specs=[pl.BlockSpec((tm, k), lambda j, i: (i, 0)),
                  pl.BlockSpec((k, tn), lambda j, i: (0, j + oa)),
                  pl.BlockSpec((k, tn), lambda j, i: (0, j + ob))],
        out_specs=pl.BlockSpec((tm, tn), lambda j, i: (i, j)),
        out_shape=jax.ShapeDtypeStruct((m, n), F32),
        scratch_shapes=[pltpu.VMEM((k, tn), BF16), pltpu.VMEM((k, tn), BF16)],
        compiler_params=_params("parallel", "arbitrary"),
        name="glu_proj",
    )(u, w_in, w_in)


def _moba_gate_kernel(q_ref, k_ref, o_ref, *, tq):
    s = k_ref.shape[0]
    nb = s // MOBA_BLOCK
    blk = lax.broadcasted_iota(jnp.int32, (nb, tq), 0)
    qpos = pl.program_id(1) * tq + lax.broadcasted_iota(jnp.int32, (nb, tq), 1)
    past = blk < qpos // MOBA_BLOCK
    for a in range(o_ref.shape[0]):
        cols = slice(a * HEAD_DIM, (a + 1) * HEAD_DIM)
        k = k_ref[:, cols].astype(F32).reshape(nb, MOBA_BLOCK, HEAD_DIM)
        k_mean = jnp.sum(k, axis=1) * (1.0 / MOBA_BLOCK)
        q = q_ref[:, cols]
        gate = jnp.zeros((nb, tq), F32)
        rest = k_mean
        for _ in range(3):
            part = rest.astype(BF16)
            gate = gate + lax.dot_general(part, q, (((1,), (1,)), ((), ())), preferred_element_type=F32)
            rest = rest - part.astype(F32)
        gate = jnp.where(past, gate, NEG_INF)
        picked = jnp.zeros((nb, tq), jnp.bool_)
        for _ in range(min(MOBA_TOPK, nb)):
            top = jnp.max(gate, axis=0, keepdims=True)
            first = jnp.min(jnp.where(gate == top, blk, nb), axis=0, keepdims=True)
            hit = blk == first
            picked = jnp.logical_or(picked, hit)
            gate = jnp.where(hit, -jnp.inf, gate)
        o_ref[a] = jnp.where(jnp.logical_and(picked, past), 0.0, NEG_INF)


def _moba_gate(qk, tq=4096, heads_per_step=4):
    s = qk.shape[0]
    nb = s // MOBA_BLOCK
    groups = ATTN_HEADS // heads_per_step
    width = heads_per_step * HEAD_DIM
    return pl.pallas_call(
        functools.partial(_moba_gate_kernel, tq=tq),
        grid=(groups, s // tq),
        in_specs=[pl.BlockSpec((tq, width), lambda h, i: (i, h)),
                  pl.BlockSpec((s, width), lambda h, i: (0, groups + h))],
        out_specs=pl.BlockSpec((heads_per_step, nb, tq), lambda h, i: (h, 0, i)),
        out_shape=jax.ShapeDtypeStruct((ATTN_HEADS, nb, s), F32),
        compiler_params=_params("parallel", "arbitrary"),
        name="moba_gate",
    )(qk, qk)


SCORE_GROUP = 8
PROB_GROUP = 4
HEADS_PER_STEP = 4
LOG2E = 1.4426950408889634


def _moba_attn_kernel(zslope_ref, q_ref, k_ref, kb_ref, vt_ref, bias_ref, o_ref, s_ref):
    qb = pl.program_id(1)
    bs = MOBA_BLOCK
    hd = HEAD_DIM
    c = (hd ** -0.5) * LOG2E
    heads = range(HEADS_PER_STEP)
    zslope = [zslope_ref[pl.program_id(0) * HEADS_PER_STEP + a] for a in heads]
    nt = (((1,), (1,)), ((), ()))
    ones3 = (lax.broadcasted_iota(jnp.int32, (bs, hd), 1) < 3).astype(BF16)
    q2 = [jnp.concatenate([q_ref[:, a * hd:(a + 1) * hd], ones3], axis=1) for a in heads]

    def scores(a, row0, n_rows):
        rows = pl.ds(pl.multiple_of(row0, bs), n_rows)
        k2 = jnp.concatenate([k_ref[rows, a * hd:(a + 1) * hd], kb_ref[a, 0:n_rows, :]], axis=1)
        return lax.dot_general(k2, q2[a], nt, preferred_element_type=F32)

    def pick(a, kb):
        return zslope[a] * (kb - qb).astype(F32) + bias_ref[a, pl.ds(kb, 1), :]

    causal = lax.broadcasted_iota(jnp.int32, (bs, bs), 0) <= lax.broadcasted_iota(jnp.int32, (bs, bs), 1)
    s_own = [jnp.where(causal, scores(a, qb * bs, bs), NEG_INF) for a in heads]
    m = tuple(c * jnp.max(s_own[a], axis=0, keepdims=True) for a in heads)

    def pass1(g, m):
        m = list(m)
        for a in heads:
            s = scores(a, g * (SCORE_GROUP * bs), SCORE_GROUP * bs)
            s_ref[a, pl.ds(pl.multiple_of(g * (SCORE_GROUP * bs), bs), SCORE_GROUP * bs), :] = s
            for b in range(SCORE_GROUP):
                kb = g * SCORE_GROUP + b
                top = jnp.max(s[b * bs:(b + 1) * bs], axis=0, keepdims=True)
                m[a] = jnp.maximum(m[a], c * top + pick(a, kb))
        return tuple(m)

    m = lax.fori_loop(0, (qb + SCORE_GROUP - 1) // SCORE_GROUP, pass1, m)

    acc = tuple(_dot(vt_ref[a, qb], jnp.exp2(c * s_own[a] - m[a]).astype(BF16)) for a in heads)

    def pass2(g, acc):
        acc = list(acc)
        for b in range(PROB_GROUP):
            kb = g * PROB_GROUP + b
            for a in heads:
                s = s_ref[a, pl.ds(pl.multiple_of(kb * bs, bs), bs), :]
                w = pick(a, kb) - m[a]
                acc[a] = acc[a] + _dot(vt_ref[a, kb], jnp.exp2(c * s + w).astype(BF16))
        return tuple(acc)

    acc = lax.fori_loop(0, (qb + PROB_GROUP - 1) // PROB_GROUP, pass2, acc)
    for a in heads:
        o_ref[:, a * hd:(a + 1) * hd] = (acc[a][0:hd] / acc[a][hd:hd + 1]).T.astype(o_ref.dtype)


def _moba_attention(qk, vt, bias):
    s = qk.shape[0]
    nb = s // MOBA_BLOCK
    assert nb % SCORE_GROUP == 0 and SCORE_GROUP % PROB_GROUP == 0
    scale = HEAD_DIM ** -0.5
    slopes = np.exp2(-8.0 * np.arange(1, ATTN_HEADS + 1, dtype=np.float32) / ATTN_HEADS).astype(np.float32)
    zslope = jnp.asarray(slopes * np.float32(LOG2E * MOBA_BLOCK))
    rest = (slopes[:, None] * np.arange(MOBA_BLOCK, dtype=np.float32)[None, :] / np.float32(scale)).astype(np.float32)
    cols = []
    for _ in range(3):
        part = rest.astype(BF16)
        cols.append(part)
        rest = rest - part.astype(np.float32)
    kbias = np.zeros((ATTN_HEADS, MOBA_BLOCK, HEAD_DIM), BF16)
    kbias[:, :, 0:3] = np.stack(cols, axis=-1)
    kbias = jnp.asarray(np.tile(kbias, (1, SCORE_GROUP, 1)))
    hps = HEADS_PER_STEP
    groups = ATTN_HEADS // hps
    once = pl.Buffered(1)
    return pl.pallas_call(
        _moba_attn_kernel,
        grid=(groups, nb),
        in_specs=[pl.BlockSpec(memory_space=pltpu.SMEM),
                  pl.BlockSpec((MOBA_BLOCK, hps * HEAD_DIM), lambda h, i: (i, h)),
                  pl.BlockSpec((s, hps * HEAD_DIM), lambda h, i: (0, groups + h), pipeline_mode=once),
                  pl.BlockSpec((hps, SCORE_GROUP * MOBA_BLOCK, HEAD_DIM), lambda h, i: (h, 0, 0), pipeline_mode=once),
                  pl.BlockSpec((hps, nb, HEAD_DIM + ONES_ROWS, MOBA_BLOCK), lambda h, i: (h, 0, 0, 0),
                               pipeline_mode=once),
                  pl.BlockSpec((hps, nb, MOBA_BLOCK), lambda h, i: (h, 0, i))],
        out_specs=pl.BlockSpec((MOBA_BLOCK, hps * HEAD_DIM), lambda h, i: (i, h)),
        out_shape=jax.ShapeDtypeStruct((s, ATTN_HEADS * HEAD_DIM), BF16),
        scratch_shapes=[pltpu.VMEM((hps, s, MOBA_BLOCK), F32)],
        compiler_params=_params("parallel", "arbitrary"),
        name="moba_attention",
    )(zslope, qk, qk, kbias, vt, bias)


def _conv_ln_silu_kernel(x_ref, halo_ref, w_ref, b_ref, g_ref, beta_ref, o_ref, win_ref, y_ref, *, taps):
    i = pl.program_id(0)
    j = pl.program_id(1)
    nj = y_ref.shape[0]
    t, ct = x_ref.shape
    hr = halo_ref.shape[0]
    win_ref[0, 0:hr, :] = jnp.where(i == 0, 0.0, halo_ref[...])
    win_ref[0, hr:, :] = x_ref[...]
    span = hr + t - F32_SUBLANES
    for p in range(1, F32_SUBLANES):
        win_ref[p, 0:span, :] = win_ref[0, pl.ds(p, span), :]
    for c in range(ct // LANES):
        lanes = slice(c * LANES, (c + 1) * LANES)
        acc = jnp.broadcast_to(b_ref[:, lanes], (t, LANES))
        for k in range(taps):
            a, p = divmod(hr - (taps - 1) + k, F32_SUBLANES)
            acc = acc + w_ref[k:k + 1, lanes] * win_ref[p, F32_SUBLANES * a:F32_SUBLANES * a + t, lanes]
        y_ref[j, :, lanes] = acc

    @pl.when(j == nj - 1)
    def _():
        n = nj * ct
        mu = sum(jnp.sum(y_ref[jj], axis=-1, keepdims=True) for jj in range(nj)) * (1.0 / n)
        var = sum(jnp.sum(jnp.square(y_ref[jj] - mu), axis=-1, keepdims=True) for jj in range(nj)) * (1.0 / n)
        inv = lax.rsqrt(var + LN_EPS)
        for jj in range(nj):
            cols = slice(jj * ct, (jj + 1) * ct)
            y = (y_ref[jj] - mu) * inv * g_ref[:, cols] + beta_ref[:, cols]
            o_ref[:, cols] = (y * _sigmoid(y)).astype(o_ref.dtype)


def _conv_ln_silu(x, w, b, g, beta, rows=256, cols=512):
    s, c = x.shape
    taps = w.shape[0]
    assert taps - 1 <= CONV_HALO_ROWS
    per = rows // CONV_HALO_ROWS
    nj = c // cols
    return pl.pallas_call(
        functools.partial(_conv_ln_silu_kernel, taps=taps),
        grid=(s // rows, nj),
        in_specs=[pl.BlockSpec((rows, cols), lambda i, j: (i, j)),
                  pl.BlockSpec((CONV_HALO_ROWS, cols), lambda i, j: (jnp.maximum(i * per - 1, 0), j)),
                  pl.BlockSpec((taps, cols), lambda i, j: (0, j)),
                  pl.BlockSpec((1, cols), lambda i, j: (0, j)),
                  pl.BlockSpec((1, c), lambda i, j: (0, 0)),
                  pl.BlockSpec((1, c), lambda i, j: (0, 0))],
        out_specs=pl.BlockSpec((rows, c), lambda i, j: (i, 0)),
        out_shape=jax.ShapeDtypeStruct((s, c), BF16),
        scratch_shapes=[pltpu.VMEM((F32_SUBLANES, rows + CONV_HALO_ROWS, cols), F32),
                        pltpu.VMEM((nj, rows, cols), F32)],
        compiler_params=_params("parallel", "arbitrary"),
        name="conformer_conv_ln_silu",
    )(x, x, w, b.reshape(1, c), g.reshape(1, c), beta.reshape(1, c))


def _merge_kernel(u_ref, a_ref, c_ref, wga_ref, wgc_ref, wpa_ref, wpc_ref, o_ref, bga_ref, bgc_ref, bpa_ref, bpc_ref):
    _cast_weights_once([wga_ref, wgc_ref, wpa_ref, wpc_ref], [bga_ref, bgc_ref, bpa_ref, bpc_ref])
    u = u_ref[...]
    ga = _sigmoid(_dot(u, bga_ref[...]))
    gc = _sigmoid(_dot(u, bgc_ref[...]))
    o_ref[...] = (ga * _dot(a_ref[...], bpa_ref[...]) + gc * _dot(c_ref[...], bpc_ref[...])).astype(o_ref.dtype)


def _merge(u, attn, cact, w_in, col_ga, col_gc, w_pa, w_pc, tm, tn):
    m, d = u.shape
    n = w_pa.shape[1]
    oa, oc = col_ga // tn, col_gc // tn
    return pl.pallas_call(
        _merge_kernel,
        grid=(n // tn, m // tm),
        in_specs=[pl.BlockSpec((tm, d), lambda j, i: (i, 0)),
                  pl.BlockSpec((tm, attn.shape[1]), lambda j, i: (i, 0)),
                  pl.BlockSpec((tm, cact.shape[1]), lambda j, i: (i, 0)),
                  pl.BlockSpec((d, tn), lambda j, i: (0, j + oa)),
                  pl.BlockSpec((d, tn), lambda j, i: (0, j + oc)),
                  pl.BlockSpec((w_pa.shape[0], tn), lambda j, i: (0, j)),
                  pl.BlockSpec((w_pc.shape[0], tn), lambda j, i: (0, j))],
        out_specs=pl.BlockSpec((tm, tn), lambda j, i: (i, j)),
        out_shape=jax.ShapeDtypeStruct((m, n), BF16),
        scratch_shapes=[pltpu.VMEM((d, tn), BF16), pltpu.VMEM((d, tn), BF16),
                        pltpu.VMEM((w_pa.shape[0], tn), BF16), pltpu.VMEM((w_pc.shape[0], tn), BF16)],
        compiler_params=_params("parallel", "arbitrary"),
        name="gated_merge",
    )(u, attn, cact, w_in, w_in, w_pa, w_pc)


def _ffn_up_kernel(f_ref, fh_ref, wg_ref, wv_ref, cwg_ref, cwv_ref, cbg_ref, cbv_ref, o_ref, win_ref, bg_ref, bv_ref):
    _cast_weights_once([wg_ref, wv_ref], [bg_ref, bv_ref])
    tm = f_ref.shape[0]
    hr = fh_ref.shape[0]
    taps = cwg_ref.shape[0]
    keep = (pl.program_id(1) > 0).astype(F32)
    f = f_ref[...]
    fh = fh_ref[...]

    def conv(z, w_ref, cw_ref, cb_ref):
        up = _dot(f, w_ref[...])
        win_ref[z, 0:hr, :] = _dot(fh, w_ref[...]) * keep
        win_ref[z, hr:, :] = up
        y = cw_ref[taps - 1:taps, :] * up + cb_ref[...]
        for k in range(taps - 1):
            y = y + cw_ref[k:k + 1, :] * win_ref[z, pl.ds(hr - (taps - 1) + k, tm), :]
        return y

    g = conv(0, bg_ref, cwg_ref, cbg_ref)
    act = g * _sigmoid(g)
    v = conv(1, bv_ref, cwv_ref, cbv_ref)
    o_ref[...] = (act * v).astype(o_ref.dtype)


def _ffn_up(f, w_up, w_dw, b_dw, d_ff, tm, tn):
    m, d = f.shape
    taps = w_dw.shape[0]
    off = d_ff // tn
    per = tm // BF16_SUBLANES
    return pl.pallas_call(
        _ffn_up_kernel,
        grid=(d_ff // tn, m // tm),
        in_specs=[pl.BlockSpec((tm, d), lambda j, i: (i, 0)),
                  pl.BlockSpec((BF16_SUBLANES, d), lambda j, i: (jnp.maximum(i * per - 1, 0), 0)),
                  pl.BlockSpec((d, tn), lambda j, i: (0, j)),
                  pl.BlockSpec((d, tn), lambda j, i: (0, j + off)),
                  pl.BlockSpec((taps, tn), lambda j, i: (0, j)),
                  pl.BlockSpec((taps, tn), lambda j, i: (0, j + off)),
                  pl.BlockSpec((1, tn), lambda j, i: (0, j)),
                  pl.BlockSpec((1, tn), lambda j, i: (0, j + off))],
        out_specs=pl.BlockSpec((tm, tn), lambda j, i: (i, j)),
        out_shape=jax.ShapeDtypeStruct((m, d_ff), BF16),
        scratch_shapes=[pltpu.VMEM((2, tm + BF16_SUBLANES, tn), F32),
                        pltpu.VMEM((d, tn), BF16), pltpu.VMEM((d, tn), BF16)],
        compiler_params=_params("parallel", "arbitrary"),
        name="convffn_up",
    )(f, f, w_up, w_up, w_dw, w_dw, b_dw.reshape(1, -1), b_dw.reshape(1, -1))


def kernel(x, g_mix, w_in, w_conv_dw, b_conv_dw, ln_conv_g, ln_conv_b, w_proj_attn, w_proj_conv, w_out,
           g_ffn, w_up, w_ffn_dw, b_ffn_dw, w_down, g_final):
    b, s, d = x.shape
    assert b == 1, "single-sequence layer"
    attn_w = ATTN_HEADS * HEAD_DIM
    conv_c = w_conv_dw.shape[1]
    d_ff = w_down.shape[0]
    col_glu_a = 3 * attn_w
    col_glu_b = col_glu_a + conv_c
    col_ga = col_glu_b + conv_c
    col_gc = col_ga + d

    x2 = x.reshape(s, d)
    u = _rmsnorm(x2, g_mix, BF16, "rmsnorm_mix")
    qk = _matmul(u, w_in, 2 * attn_w, 0, 1024, 512, BF16, "qk_proj")
    vt = _v_proj(u, w_in, 2 * attn_w, 1024, 512)
    c0 = _glu(u, w_in, col_glu_a, col_glu_b, conv_c, 1024, 256)

    bias = _moba_gate(qk)
    attn = _moba_attention(qk, vt, bias)

    cact = _conv_ln_silu(c0, w_conv_dw, b_conv_dw, ln_conv_g, ln_conv_b)

    merged = _merge(u, attn, cact, w_in, col_ga, col_gc, w_proj_attn, w_proj_conv, 512, 256)
    h = _matmul(merged, w_out, d, 0, 1024, 512, F32, "out_proj", residual=x2)

    f = _rmsnorm(h, g_ffn, BF16, "rmsnorm_ffn")
    g = _ffn_up(f, w_up, w_ffn_dw, b_ffn_dw, d_ff, 1024, 256)
    h2 = _matmul_rows_resident(g, w_down.astype(BF16), 512, 512, "ffn_down", residual=h)
    out = _rmsnorm(h2, g_final, F32, "rmsnorm_final")
    return out.reshape(b, s, d)
```

```python
import functools

import jax
import jax.numpy as jnp
import numpy as np
from jax import lax
from jax.experimental import pallas as pl
from jax.experimental.pallas import tpu as pltpu

ATTN_HEADS = 16
HEAD_DIM = 128
MOBA_BLOCK = 256
MOBA_TOPK = 3
NORM_EPS = 1e-6
LN_EPS = 1e-5
NEG_INF = -1e30

F32 = jnp.float32
BF16 = jnp.bfloat16

V7X_VMEM_BYTES = 64 * 1024 * 1024
VMEM_LIMIT_BYTES = V7X_VMEM_BYTES - 8 * 1024 * 1024
LANES = 128
F32_SUBLANES = 8
BF16_SUBLANES = 16
CONV_HALO_ROWS = 32
ONES_ROWS = 16

TILES = {
    "qk_proj": (1024, 512),
    "v_proj": (1024, 512),
    "glu_proj": (1024, 256),
    "gated_merge": (512, 256),
    "out_proj": (1024, 512),
    "convffn_up": (1024, 256),
    "ffn_down": (512, 512),
}


def _params(*sem):
    return pltpu.CompilerParams(dimension_semantics=sem, vmem_limit_bytes=VMEM_LIMIT_BYTES)


def _dot(a, b):
    return jnp.dot(a, b, preferred_element_type=F32)


def _sigmoid(x):
    return 0.5 * jnp.tanh(0.5 * x) + 0.5


def _cast_weights_once(w_refs, wb_refs):
    @pl.when(pl.program_id(1) == 0)
    def _():
        for w_ref, wb_ref in zip(w_refs, wb_refs):
            wb_ref[...] = w_ref[...].astype(BF16)


def _rmsnorm_kernel(x_ref, g_ref, o_ref):
    x = x_ref[...]
    ms = jnp.mean(x * x, axis=-1, keepdims=True)
    o_ref[...] = (x * lax.rsqrt(ms + NORM_EPS) * g_ref[...]).astype(o_ref.dtype)


def _rmsnorm(x, g, out_dtype, name, rows=512):
    s, d = x.shape
    return pl.pallas_call(
        _rmsnorm_kernel,
        grid=(s // rows,),
        in_specs=[pl.BlockSpec((rows, d), lambda i: (i, 0)),
                  pl.BlockSpec((1, d), lambda i: (0, 0))],
        out_specs=pl.BlockSpec((rows, d), lambda i: (i, 0)),
        out_shape=jax.ShapeDtypeStruct((s, d), out_dtype),
        compiler_params=_params("parallel"),
        name=name,
    )(x, g.reshape(1, d))


def _mm_kernel(a_ref, w_ref, o_ref, wb_ref):
    _cast_weights_once([w_ref], [wb_ref])
    o_ref[...] = _dot(a_ref[...], wb_ref[...]).astype(o_ref.dtype)


def _mm_residual_kernel(a_ref, w_ref, r_ref, o_ref, wb_ref):
    _cast_weights_once([w_ref], [wb_ref])
    o_ref[...] = r_ref[...] + _dot(a_ref[...], wb_ref[...])


def _matmul(a, w, n, col0, tm, tn, out_dtype, name, residual=None):
    m, k = a.shape
    off = col0 // tn
    in_specs = [pl.BlockSpec((tm, k), lambda j, i: (i, 0)),
                pl.BlockSpec((k, tn), lambda j, i: (0, j + off))]
    args = [a, w]
    kern = _mm_kernel
    if residual is not None:
        in_specs.append(pl.BlockSpec((tm, tn), lambda j, i: (i, j)))
        args.append(residual)
        kern = _mm_residual_kernel
    return pl.pallas_call(
        kern,
        grid=(n // tn, m // tm),
        in_specs=in_specs,
        out_specs=pl.BlockSpec((tm, tn), lambda j, i: (i, j)),
        out_shape=jax.ShapeDtypeStruct((m, n), out_dtype),
        scratch_shapes=[pltpu.VMEM((k, tn), BF16)],
        compiler_params=_params("parallel", "arbitrary"),
        name=name,
    )(*args)


def _v_proj_kernel(a_ref, w_ref, o_ref, wb_ref):
    _cast_weights_once([w_ref], [wb_ref])
    acc = _dot(a_ref[...], wb_ref[...])
    n_heads, n_blocks = o_ref.shape[0], o_ref.shape[1]
    ones = jnp.ones((ONES_ROWS, MOBA_BLOCK), o_ref.dtype)
    for hh in range(n_heads):
        for rb in range(n_blocks):
            blk = acc[rb * MOBA_BLOCK:(rb + 1) * MOBA_BLOCK, hh * HEAD_DIM:(hh + 1) * HEAD_DIM]
            o_ref[hh, rb, 0:HEAD_DIM, :] = blk.T.astype(o_ref.dtype)
            o_ref[hh, rb, HEAD_DIM:, :] = ones


def _v_proj(a, w, col0, tm, tn):
    m, k = a.shape
    off = col0 // tn
    nb = m // MOBA_BLOCK
    return pl.pallas_call(
        _v_proj_kernel,
        grid=(ATTN_HEADS * HEAD_DIM // tn, m // tm),
        in_specs=[pl.BlockSpec((tm, k), lambda j, i: (i, 0)),
                  pl.BlockSpec((k, tn), lambda j, i: (0, j + off))],
        out_specs=pl.BlockSpec((tn // HEAD_DIM, tm // MOBA_BLOCK, HEAD_DIM + ONES_ROWS, MOBA_BLOCK),
                               lambda j, i: (j, i, 0, 0)),
        out_shape=jax.ShapeDtypeStruct((ATTN_HEADS, nb, HEAD_DIM + ONES_ROWS, MOBA_BLOCK), BF16),
        scratch_shapes=[pltpu.VMEM((k, tn), BF16)],
        compiler_params=_params("parallel", "arbitrary"),
        name="v_proj",
    )(a, w)


def _mm_bf16_residual_kernel(a_ref, w_ref, r_ref, o_ref):
    o_ref[...] = r_ref[...] + _dot(a_ref[...], w_ref[...])


def _matmul_rows_resident(a, w, tm, tn, name, residual):
    m, k = a.shape
    n = w.shape[1]
    return pl.pallas_call(
        _mm_bf16_residual_kernel,
        grid=(m // tm, n // tn),
        in_specs=[pl.BlockSpec((tm, k), lambda i, j: (i, 0)),
                  pl.BlockSpec((k, tn), lambda i, j: (0, j)),
                  pl.BlockSpec((tm, tn), lambda i, j: (i, j))],
        out_specs=pl.BlockSpec((tm, tn), lambda i, j: (i, j)),
        out_shape=jax.ShapeDtypeStruct((m, n), F32),
        compiler_params=_params("parallel", "arbitrary"),
        name=name,
    )(a, w, residual)


def _glu_kernel(u_ref, wa_ref, wb_ref, o_ref, wab_ref, wbb_ref):
    _cast_weights_once([wa_ref, wb_ref], [wab_ref, wbb_ref])
    u = u_ref[...]
    o_ref[...] = _dot(u, wab_ref[...]) * _sigmoid(_dot(u, wbb_ref[...]))


def _glu(u, w_in, col_a, col_b, n, tm, tn):
    m, k = u.shape
    oa, ob = col_a // tn, col_b // tn
    return pl.pallas_call(
        _glu_kernel,
        grid=(n // tn, m // tm),
        in_specs=[pl.BlockSpec((tm, k), lambda j, i: (i, 0)),
                  pl.BlockSpec((k, tn), lambda j, i: (0, j + oa)),
                  pl.BlockSpec((k, tn), lambda j, i: (0, j + ob))],
        out_specs=pl.BlockSpec((tm, tn), lambda j, i: (i, j)),
        out_shape=jax.ShapeDtypeStruct((m, n), F32),
        scratch_shapes=[pltpu.VMEM((k, tn), BF16), pltpu.VMEM((k, tn), BF16)],
        compiler_params=_params("parallel", "arbitrary"),
        name="glu_proj",
    )(u, w_in, w_in)


def _moba_gate_kernel(q_ref, k_ref, o_ref, *, tq):
    s = k_ref.shape[0]
    nb = s // MOBA_BLOCK
    blk = lax.broadcasted_iota(jnp.int32, (nb, tq), 0)
    qpos = pl.program_id(1) * tq + lax.broadcasted_iota(jnp.int32, (nb, tq), 1)
    past = blk < qpos // MOBA_BLOCK
    for a in range(o_ref.shape[0]):
        cols = slice(a * HEAD_DIM, (a + 1) * HEAD_DIM)
        k = k_ref[:, cols].astype(F32).reshape(nb, MOBA_BLOCK, HEAD_DIM)
        k_mean = jnp.sum(k, axis=1) * (1.0 / MOBA_BLOCK)
        q = q_ref[:, cols]
        gate = jnp.zeros((nb, tq), F32)
        rest = k_mean
        for _ in range(3):
            part = rest.astype(BF16)
            gate = gate + lax.dot_general(part, q, (((1,), (1,)), ((), ())), preferred_element_type=F32)
            rest = rest - part.astype(F32)
        gate = jnp.where(past, gate, NEG_INF)
        picked = jnp.zeros((nb, tq), jnp.bool_)
        for _ in range(min(MOBA_TOPK, nb)):
            top = jnp.max(gate, axis=0, keepdims=True)
            first = jnp.min(jnp.where(gate == top, blk, nb), axis=0, keepdims=True)
            hit = blk == first
            picked = jnp.logical_or(picked, hit)
            gate = jnp.where(hit, -jnp.inf, gate)
        o_ref[a] = jnp.where(jnp.logical_and(picked, past), 0.0, NEG_INF)


def _moba_gate(qk, tq=4096, heads_per_step=4):
    s = qk.shape[0]
    nb = s // MOBA_BLOCK
    groups = ATTN_HEADS // heads_per_step
    width = heads_per_step * HEAD_DIM
    return pl.pallas_call(
        functools.partial(_moba_gate_kernel, tq=tq),
        grid=(groups, s // tq),
        in_specs=[pl.BlockSpec((tq, width), lambda h, i: (i, h)),
                  pl.BlockSpec((s, width), lambda h, i: (0, groups + h))],
        out_specs=pl.BlockSpec((heads_per_step, nb, tq), lambda h, i: (h, 0, i)),
        out_shape=jax.ShapeDtypeStruct((ATTN_HEADS, nb, s), F32),
        compiler_params=_params("parallel", "arbitrary"),
        name="moba_gate",
    )(qk, qk)


SCORE_GROUP = 8
PROB_GROUP = 4
HEADS_PER_STEP = 4
LOG2E = 1.4426950408889634


def _moba_attn_kernel(zslope_ref, q_ref, k_ref, kb_ref, vt_ref, bias_ref, o_ref, s_ref):
    qb = pl.program_id(1)
    bs = MOBA_BLOCK
    hd = HEAD_DIM
    c = (hd ** -0.5) * LOG2E
    heads = range(HEADS_PER_STEP)
    zslope = [zslope_ref[pl.program_id(0) * HEADS_PER_STEP + a] for a in heads]
    nt = (((1,), (1,)), ((), ()))
    ones3 = (lax.broadcasted_iota(jnp.int32, (bs, hd), 1) < 3).astype(BF16)
    q2 = [jnp.concatenate([q_ref[:, a * hd:(a + 1) * hd], ones3], axis=1) for a in heads]

    def scores(a, row0, n_rows):
        rows = pl.ds(pl.multiple_of(row0, bs), n_rows)
        k2 = jnp.concatenate([k_ref[rows, a * hd:(a + 1) * hd], kb_ref[a, 0:n_rows, :]], axis=1)
        return lax.dot_general(k2, q2[a], nt, preferred_element_type=F32)

    def pick(a, kb):
        return zslope[a] * (kb - qb).astype(F32) + bias_ref[a, pl.ds(kb, 1), :]

    causal = lax.broadcasted_iota(jnp.int32, (bs, bs), 0) <= lax.broadcasted_iota(jnp.int32, (bs, bs), 1)
    z_own = [jnp.where(causal, c * scores(a, qb * bs, bs), NEG_INF) for a in heads]
    m = tuple(jnp.max(z_own[a], axis=0, keepdims=True) for a in heads)

    def pass1(g, m):
        m = list(m)
        for a in heads:
            z = c * scores(a, g * (SCORE_GROUP * bs), SCORE_GROUP * bs)
            s_ref[a, pl.ds(pl.multiple_of(g * (SCORE_GROUP * bs), bs), SCORE_GROUP * bs), :] = z
            for b in range(SCORE_GROUP):
                kb = g * SCORE_GROUP + b
                top = jnp.max(z[b * bs:(b + 1) * bs], axis=0, keepdims=True)
                m[a] = jnp.maximum(m[a], top + pick(a, kb))
        return tuple(m)

    m = lax.fori_loop(0, (qb + SCORE_GROUP - 1) // SCORE_GROUP, pass1, m)

    acc = tuple(_dot(vt_ref[a, qb], jnp.exp2(z_own[a] - m[a]).astype(BF16)) for a in heads)

    def pass2(g, acc):
        acc = list(acc)
        for b in range(PROB_GROUP):
            kb = g * PROB_GROUP + b
            for a in heads:
                z = s_ref[a, pl.ds(pl.multiple_of(kb * bs, bs), bs), :]
                w = pick(a, kb) - m[a]
                acc[a] = acc[a] + _dot(vt_ref[a, kb], jnp.exp2(z + w).astype(BF16))
        return tuple(acc)

    acc = lax.fori_loop(0, (qb + PROB_GROUP - 1) // PROB_GROUP, pass2, acc)
    for a in heads:
        o_ref[:, a * hd:(a + 1) * hd] = (acc[a][0:hd] / acc[a][hd:hd + 1]).T.astype(o_ref.dtype)


def _moba_attention(qk, vt, bias):
    s = qk.shape[0]
    nb = s // MOBA_BLOCK
    assert nb % SCORE_GROUP == 0 and SCORE_GROUP % PROB_GROUP == 0
    scale = HEAD_DIM ** -0.5
    slopes = np.exp2(-8.0 * np.arange(1, ATTN_HEADS + 1, dtype=np.float32) / ATTN_HEADS).astype(np.float32)
    zslope = jnp.asarray(slopes * np.float32(LOG2E * MOBA_BLOCK))
    rest = (slopes[:, None] * np.arange(MOBA_BLOCK, dtype=np.float32)[None, :] / np.float32(scale)).astype(np.float32)
    cols = []
    for _ in range(3):
        part = rest.astype(BF16)
        cols.append(part)
        rest = rest - part.astype(np.float32)
    kbias = np.zeros((ATTN_HEADS, MOBA_BLOCK, HEAD_DIM), BF16)
    kbias[:, :, 0:3] = np.stack(cols, axis=-1)
    kbias = jnp.asarray(np.tile(kbias, (1, SCORE_GROUP, 1)))
    hps = HEADS_PER_STEP
    groups = ATTN_HEADS // hps
    once = pl.Buffered(1)
    return pl.pallas_call(
        _moba_attn_kernel,
        grid=(groups, nb),
        in_specs=[pl.BlockSpec(memory_space=pltpu.SMEM),
                  pl.BlockSpec((MOBA_BLOCK, hps * HEAD_DIM), lambda h, i: (i, h)),
                  pl.BlockSpec((s, hps * HEAD_DIM), lambda h, i: (0, groups + h), pipeline_mode=once),
                  pl.BlockSpec((hps, SCORE_GROUP * MOBA_BLOCK, HEAD_DIM), lambda h, i: (h, 0, 0), pipeline_mode=once),
                  pl.BlockSpec((hps, nb, HEAD_DIM + ONES_ROWS, MOBA_BLOCK), lambda h, i: (h, 0, 0, 0),
                               pipeline_mode=once),
                  pl.BlockSpec((hps, nb, MOBA_BLOCK), lambda h, i: (h, 0, i))],
        out_specs=pl.BlockSpec((MOBA_BLOCK, hps * HEAD_DIM), lambda h, i: (i, h)),
        out_shape=jax.ShapeDtypeStruct((s, ATTN_HEADS * HEAD_DIM), BF16),
        scratch_shapes=[pltpu.VMEM((hps, s, MOBA_BLOCK), F32)],
        compiler_params=_params("parallel", "arbitrary"),
        name="moba_attention",
    )(zslope, qk, qk, kbias, vt, bias)


def _conv_ln_silu_kernel(x_ref, halo_ref, w_ref, b_ref, g_ref, beta_ref, o_ref, win_ref, y_ref, *, taps):
    i = pl.program_id(0)
    j = pl.program_id(1)
    nj = y_ref.shape[0]
    t, ct = x_ref.shape
    hr = halo_ref.shape[0]
    win_ref[0, 0:hr, :] = jnp.where(i == 0, 0.0, halo_ref[...])
    win_ref[0, hr:, :] = x_ref[...]
    span = hr + t - F32_SUBLANES
    for p in range(1, F32_SUBLANES):
        win_ref[p, 0:span, :] = win_ref[0, pl.ds(p, span), :]
    for c in range(ct // LANES):
        lanes = slice(c * LANES, (c + 1) * LANES)
        acc = jnp.broadcast_to(b_ref[:, lanes], (t, LANES))
        for k in range(taps):
            a, p = divmod(hr - (taps - 1) + k, F32_SUBLANES)
            acc = acc + w_ref[k:k + 1, lanes] * win_ref[p, F32_SUBLANES * a:F32_SUBLANES * a + t, lanes]
        y_ref[j, :, lanes] = acc

    @pl.when(j == nj - 1)
    def _():
        n = nj * ct
        mu = sum(jnp.sum(y_ref[jj], axis=-1, keepdims=True) for jj in range(nj)) * (1.0 / n)
        var = sum(jnp.sum(jnp.square(y_ref[jj] - mu), axis=-1, keepdims=True) for jj in range(nj)) * (1.0 / n)
        inv = lax.rsqrt(var + LN_EPS)
        for jj in range(nj):
            cols = slice(jj * ct, (jj + 1) * ct)
            y = (y_ref[jj] - mu) * inv * g_ref[:, cols] + beta_ref[:, cols]
            o_ref[:, cols] = (y * _sigmoid(y)).astype(o_ref.dtype)


def _conv_ln_silu(x, w, b, g, beta, rows=256, cols=512):
    s, c = x.shape
    taps = w.shape[0]
    assert taps - 1 <= CONV_HALO_ROWS
    per = rows // CONV_HALO_ROWS
    nj = c // cols
    return pl.pallas_call(
        functools.partial(_conv_ln_silu_kernel, taps=taps),
        grid=(s // rows, nj),
        in_specs=[pl.BlockSpec((rows, cols), lambda i, j: (i, j)),
                  pl.BlockSpec((CONV_HALO_ROWS, cols), lambda i, j: (jnp.maximum(i * per - 1, 0), j)),
                  pl.BlockSpec((taps, cols), lambda i, j: (0, j)),
                  pl.BlockSpec((1, cols), lambda i, j: (0, j)),
                  pl.BlockSpec((1, c), lambda i, j: (0, 0)),
                  pl.BlockSpec((1, c), lambda i, j: (0, 0))],
        out_specs=pl.BlockSpec((rows, c), lambda i, j: (i, 0)),
        out_shape=jax.ShapeDtypeStruct((s, c), BF16),
        scratch_shapes=[pltpu.VMEM((F32_SUBLANES, rows + CONV_HALO_ROWS, cols), F32),
                        pltpu.VMEM((nj, rows, cols), F32)],
        compiler_params=_params("parallel", "arbitrary"),
        name="conformer_conv_ln_silu",
    )(x, x, w, b.reshape(1, c), g.reshape(1, c), beta.reshape(1, c))


def _merge_kernel(u_ref, a_ref, c_ref, wga_ref, wgc_ref, wpa_ref, wpc_ref, o_ref, bga_ref, bgc_ref, bpa_ref, bpc_ref):
    _cast_weights_once([wga_ref, wgc_ref, wpa_ref, wpc_ref], [bga_ref, bgc_ref, bpa_ref, bpc_ref])
    u = u_ref[...]
    ga = _sigmoid(_dot(u, bga_ref[...]))
    gc = _sigmoid(_dot(u, bgc_ref[...]))
    o_ref[...] = (ga * _dot(a_ref[...], bpa_ref[...]) + gc * _dot(c_ref[...], bpc_ref[...])).astype(o_ref.dtype)


def _merge(u, attn, cact, w_in, col_ga, col_gc, w_pa, w_pc, tm, tn):
    m, d = u.shape
    n = w_pa.shape[1]
    oa, oc = col_ga // tn, col_gc // tn
    return pl.pallas_call(
        _merge_kernel,
        grid=(n // tn, m // tm),
        in_specs=[pl.BlockSpec((tm, d), lambda j, i: (i, 0)),
                  pl.BlockSpec((tm, attn.shape[1]), lambda j, i: (i, 0)),
                  pl.BlockSpec((tm, cact.shape[1]), lambda j, i: (i, 0)),
                  pl.BlockSpec((d, tn), lambda j, i: (0, j + oa)),
                  pl.BlockSpec((d, tn), lambda j, i: (0, j + oc)),
                  pl.BlockSpec((w_pa.shape[0], tn), lambda j, i: (0, j)),
                  pl.BlockSpec((w_pc.shape[0], tn), lambda j, i: (0, j))],
        out_specs=pl.BlockSpec((tm, tn), lambda j, i: (i, j)),
        out_shape=jax.ShapeDtypeStruct((m, n), BF16),
        scratch_shapes=[pltpu.VMEM((d, tn), BF16), pltpu.VMEM((d, tn), BF16),
                        pltpu.VMEM((w_pa.shape[0], tn), BF16), pltpu.VMEM((w_pc.shape[0], tn), BF16)],
        compiler_params=_params("parallel", "arbitrary"),
        name="gated_merge",
    )(u, attn, cact, w_in, w_in, w_pa, w_pc)


def _ffn_up_kernel(f_ref, fh_ref, wg_ref, wv_ref, cwg_ref, cwv_ref, cbg_ref, cbv_ref, o_ref, win_ref, bg_ref, bv_ref):
    _cast_weights_once([wg_ref, wv_ref], [bg_ref, bv_ref])
    tm = f_ref.shape[0]
    hr = fh_ref.shape[0]
    taps = cwg_ref.shape[0]
    keep = (pl.program_id(1) > 0).astype(F32)
    f = f_ref[...]
    fh = fh_ref[...]

    def conv(z, w_ref, cw_ref, cb_ref):
        up = _dot(f, w_ref[...])
        win_ref[z, 0:hr, :] = _dot(fh, w_ref[...]) * keep
        win_ref[z, hr:, :] = up
        y = cw_ref[taps - 1:taps, :] * up + cb_ref[...]
        for k in range(taps - 1):
            y = y + cw_ref[k:k + 1, :] * win_ref[z, pl.ds(hr - (taps - 1) + k, tm), :]
        return y

    g = conv(0, bg_ref, cwg_ref, cbg_ref)
    act = g * _sigmoid(g)
    v = conv(1, bv_ref, cwv_ref, cbv_ref)
    o_ref[...] = (act * v).astype(o_ref.dtype)


def _ffn_up(f, w_up, w_dw, b_dw, d_ff, tm, tn):
    m, d = f.shape
    taps = w_dw.shape[0]
    off = d_ff // tn
    per = tm // BF16_SUBLANES
    return pl.pallas_call(
        _ffn_up_kernel,
        grid=(d_ff // tn, m // tm),
        in_specs=[pl.BlockSpec((tm, d), lambda j, i: (i, 0)),
                  pl.BlockSpec((BF16_SUBLANES, d), lambda j, i: (jnp.maximum(i * per - 1, 0), 0)),
                  pl.BlockSpec((d, tn), lambda j, i: (0, j)),
                  pl.BlockSpec((d, tn), lambda j, i: (0, j + off)),
                  pl.BlockSpec((taps, tn), lambda j, i: (0, j)),
                  pl.BlockSpec((taps, tn), lambda j, i: (0, j + off)),
                  pl.BlockSpec((1, tn), lambda j, i: (0, j)),
                  pl.BlockSpec((1, tn), lambda j, i: (0, j + off))],
        out_specs=pl.BlockSpec((tm, tn), lambda j, i: (i, j)),
        out_shape=jax.ShapeDtypeStruct((m, d_ff), BF16),
        scratch_shapes=[pltpu.VMEM((2, tm + BF16_SUBLANES, tn), F32),
                        pltpu.VMEM((d, tn), BF16), pltpu.VMEM((d, tn), BF16)],
        compiler_params=_params("parallel", "arbitrary"),
        name="convffn_up",
    )(f, f, w_up, w_up, w_dw, w_dw, b_dw.reshape(1, -1), b_dw.reshape(1, -1))


def kernel(x, g_mix, w_in, w_conv_dw, b_conv_dw, ln_conv_g, ln_conv_b, w_proj_attn, w_proj_conv, w_out,
           g_ffn, w_up, w_ffn_dw, b_ffn_dw, w_down, g_final):
    b, s, d = x.shape
    assert b == 1, "single-sequence layer"
    attn_w = ATTN_HEADS * HEAD_DIM
    conv_c = w_conv_dw.shape[1]
    d_ff = w_down.shape[0]
    col_glu_a = 3 * attn_w
    col_glu_b = col_glu_a + conv_c
    col_ga = col_glu_b + conv_c
    col_gc = col_ga + d

    x2 = x.reshape(s, d)
    u = _rmsnorm(x2, g_mix, BF16, "rmsnorm_mix")
    qk = _matmul(u, w_in, 2 * attn_w, 0, *TILES["qk_proj"], BF16, "qk_proj")
    vt = _v_proj(u, w_in, 2 * attn_w, *TILES["v_proj"])
    c0 = _glu(u, w_in, col_glu_a, col_glu_b, conv_c, *TILES["glu_proj"])

    bias = _moba_gate(qk)
    attn = _moba_attention(qk, vt, bias)

    cact = _conv_ln_silu(c0, w_conv_dw, b_conv_dw, ln_conv_g, ln_conv_b)

    merged = _merge(u, attn, cact, w_in, col_ga, col_gc, w_proj_attn, w_proj_conv, *TILES["gated_merge"])
    h = _matmul(merged, w_out, d, 0, *TILES["out_proj"], F32, "out_proj", residual=x2)

    f = _rmsnorm(h, g_ffn, BF16, "rmsnorm_ffn")
    g = _ffn_up(f, w_up, w_ffn_dw, b_ffn_dw, d_ff, *TILES["convffn_up"])
    h2 = _matmul_rows_resident(g, w_down.astype(BF16), *TILES["ffn_down"], "ffn_down", residual=h)
    out = _rmsnorm(h2, g_final, F32, "rmsnorm_final")
    return out.reshape(b, s, d)
```

```python
import functools

import jax
import jax.numpy as jnp
import numpy as np
from jax import lax
from jax.experimental import pallas as pl
from jax.experimental.pallas import tpu as pltpu

ATTN_HEADS = 16
HEAD_DIM = 128
MOBA_BLOCK = 256
MOBA_TOPK = 3
NORM_EPS = 1e-6
LN_EPS = 1e-5
NEG_INF = -1e30

F32 = jnp.float32
BF16 = jnp.bfloat16

V7X_VMEM_BYTES = 64 * 1024 * 1024
VMEM_LIMIT_BYTES = V7X_VMEM_BYTES - 8 * 1024 * 1024
LANES = 128
F32_SUBLANES = 8
BF16_SUBLANES = 16
CONV_HALO_ROWS = 32
ONES_ROWS = 16

TILES = {
    "qk_proj": (1024, 512),
    "v_proj": (1024, 512),
    "glu_proj": (1024, 256),
    "gated_merge": (512, 256),
    "out_proj": (1024, 512),
    "convffn_up": (1024, 256),
    "ffn_down": (512, 512),
}


def _params(*sem):
    return pltpu.CompilerParams(dimension_semantics=sem, vmem_limit_bytes=VMEM_LIMIT_BYTES)


def _dot(a, b):
    return jnp.dot(a, b, preferred_element_type=F32)


def _sigmoid(x):
    return 0.5 * jnp.tanh(0.5 * x) + 0.5


def _cast_weights_once(w_refs, wb_refs):
    @pl.when(pl.program_id(1) == 0)
    def _():
        for w_ref, wb_ref in zip(w_refs, wb_refs):
            wb_ref[...] = w_ref[...].astype(BF16)


def _rmsnorm_kernel(x_ref, g_ref, o_ref):
    x = x_ref[...]
    ms = jnp.mean(x * x, axis=-1, keepdims=True)
    o_ref[...] = (x * lax.rsqrt(ms + NORM_EPS) * g_ref[...]).astype(o_ref.dtype)


def _rmsnorm(x, g, out_dtype, name, rows=512):
    s, d = x.shape
    return pl.pallas_call(
        _rmsnorm_kernel,
        grid=(s // rows,),
        in_specs=[pl.BlockSpec((rows, d), lambda i: (i, 0)),
                  pl.BlockSpec((1, d), lambda i: (0, 0))],
        out_specs=pl.BlockSpec((rows, d), lambda i: (i, 0)),
        out_shape=jax.ShapeDtypeStruct((s, d), out_dtype),
        compiler_params=_params("parallel"),
        name=name,
    )(x, g.reshape(1, d))


def _mm_kernel(a_ref, w_ref, o_ref, wb_ref):
    _cast_weights_once([w_ref], [wb_ref])
    o_ref[...] = _dot(a_ref[...], wb_ref[...]).astype(o_ref.dtype)


def _mm_residual_kernel(a_ref, w_ref, r_ref, o_ref, wb_ref):
    _cast_weights_once([w_ref], [wb_ref])
    o_ref[...] = r_ref[...] + _dot(a_ref[...], wb_ref[...])


def _matmul(a, w, n, col0, tm, tn, out_dtype, name, residual=None):
    m, k = a.shape
    off = col0 // tn
    in_specs = [pl.BlockSpec((tm, k), lambda j, i: (i, 0)),
                pl.BlockSpec((k, tn), lambda j, i: (0, j + off))]
    args = [a, w]
    kern = _mm_kernel
    if residual is not None:
        in_specs.append(pl.BlockSpec((tm, tn), lambda j, i: (i, j)))
        args.append(residual)
        kern = _mm_residual_kernel
    return pl.pallas_call(
        kern,
        grid=(n // tn, m // tm),
        in_specs=in_specs,
        out_specs=pl.BlockSpec((tm, tn), lambda j, i: (i, j)),
        out_shape=jax.ShapeDtypeStruct((m, n), out_dtype),
        scratch_shapes=[pltpu.VMEM((k, tn), BF16)],
        compiler_params=_params("parallel", "arbitrary"),
        name=name,
    )(*args)


def _v_proj_kernel(a_ref, w_ref, o_ref, wb_ref):
    _cast_weights_once([w_ref], [wb_ref])
    acc = _dot(a_ref[...], wb_ref[...])
    n_heads, n_blocks = o_ref.shape[0], o_ref.shape[1]
    ones = jnp.ones((ONES_ROWS, MOBA_BLOCK), o_ref.dtype)
    for hh in range(n_heads):
        for rb in range(n_blocks):
            blk = acc[rb * MOBA_BLOCK:(rb + 1) * MOBA_BLOCK, hh * HEAD_DIM:(hh + 1) * HEAD_DIM]
            o_ref[hh, rb, 0:HEAD_DIM, :] = blk.T.astype(o_ref.dtype)
            o_ref[hh, rb, HEAD_DIM:, :] = ones


def _v_proj(a, w, col0, tm, tn):
    m, k = a.shape
    off = col0 // tn
    nb = m // MOBA_BLOCK
    return pl.pallas_call(
        _v_proj_kernel,
        grid=(ATTN_HEADS * HEAD_DIM // tn, m // tm),
        in_specs=[pl.BlockSpec((tm, k), lambda j, i: (i, 0)),
                  pl.BlockSpec((k, tn), lambda j, i: (0, j + off))],
        out_specs=pl.BlockSpec((tn // HEAD_DIM, tm // MOBA_BLOCK, HEAD_DIM + ONES_ROWS, MOBA_BLOCK),
                               lambda j, i: (j, i, 0, 0)),
        out_shape=jax.ShapeDtypeStruct((ATTN_HEADS, nb, HEAD_DIM + ONES_ROWS, MOBA_BLOCK), BF16),
        scratch_shapes=[pltpu.VMEM((k, tn), BF16)],
        compiler_params=_params("parallel", "arbitrary"),
        name="v_proj",
    )(a, w)


def _mm_bf16_residual_kernel(a_ref, w_ref, r_ref, o_ref):
    o_ref[...] = r_ref[...] + _dot(a_ref[...], w_ref[...])


def _matmul_rows_resident(a, w, tm, tn, name, residual):
    m, k = a.shape
    n = w.shape[1]
    return pl.pallas_call(
        _mm_bf16_residual_kernel,
        grid=(m // tm, n // tn),
        in_specs=[pl.BlockSpec((tm, k), lambda i, j: (i, 0)),
                  pl.BlockSpec((k, tn), lambda i, j: (0, j)),
                  pl.BlockSpec((tm, tn), lambda i, j: (i, j))],
        out_specs=pl.BlockSpec((tm, tn), lambda i, j: (i, j)),
        out_shape=jax.ShapeDtypeStruct((m, n), F32),
        compiler_params=_params("parallel", "arbitrary"),
        name=name,
    )(a, w, residual)


def _glu_kernel(u_ref, wa_ref, wb_ref, o_ref, wab_ref, wbb_ref):
    _cast_weights_once([wa_ref, wb_ref], [wab_ref, wbb_ref])
    u = u_ref[...]
    o_ref[...] = _dot(u, wab_ref[...]) * _sigmoid(_dot(u, wbb_ref[...]))


def _glu(u, w_in, col_a, col_b, n, tm, tn):
    m, k = u.shape
    oa, ob = col_a // tn, col_b // tn
    return pl.pallas_call(
        _glu_kernel,
        grid=(n // tn, m // tm),
        in_specs=[pl.BlockSpec((tm, k), lambda j, i: (i, 0)),
                  pl.BlockSpec((k, tn), lambda j, i: (0, j + oa)),
                  pl.BlockSpec((k, tn), lambda j, i: (0, j + ob))],
        out_specs=pl.BlockSpec((tm, tn), lambda j, i: (i, j)),
        out_shape=jax.ShapeDtypeStruct((m, n), F32),
        scratch_shapes=[pltpu.VMEM((k, tn), BF16), pltpu.VMEM((k, tn), BF16)],
        compiler_params=_params("parallel", "arbitrary"),
        name="glu_proj",
    )(u, w_in, w_in)


def _moba_gate_kernel(q_ref, k_ref, o_ref, *, tq):
    s = k_ref.shape[0]
    nb = s // MOBA_BLOCK
    blk = lax.broadcasted_iota(jnp.int32, (nb, tq), 0)
    qpos = pl.program_id(1) * tq + lax.broadcasted_iota(jnp.int32, (nb, tq), 1)
    past = blk < qpos // MOBA_BLOCK
    for a in range(o_ref.shape[0]):
        cols = slice(a * HEAD_DIM, (a + 1) * HEAD_DIM)
        k = k_ref[:, cols].astype(F32).reshape(nb, MOBA_BLOCK, HEAD_DIM)
        k_mean = jnp.sum(k, axis=1) * (1.0 / MOBA_BLOCK)
        q = q_ref[:, cols]
        gate = jnp.zeros((nb, tq), F32)
        rest = k_mean
        for _ in range(3):
            part = rest.astype(BF16)
            gate = gate + lax.dot_general(part, q, (((1,), (1,)), ((), ())), preferred_element_type=F32)
            rest = rest - part.astype(F32)
        gate = jnp.where(past, gate, NEG_INF)
        picked = jnp.zeros((nb, tq), jnp.bool_)
        for _ in range(min(MOBA_TOPK, nb)):
            top = jnp.max(gate, axis=0, keepdims=True)
            first = jnp.min(jnp.where(gate == top, blk, nb), axis=0, keepdims=True)
            hit = blk == first
            picked = jnp.logical_or(picked, hit)
            gate = jnp.where(hit, -jnp.inf, gate)
        o_ref[a] = jnp.where(jnp.logical_and(picked, past), 0.0, NEG_INF)


def _moba_gate(qk, tq=4096, heads_per_step=4):
    s = qk.shape[0]
    nb = s // MOBA_BLOCK
    groups = ATTN_HEADS // heads_per_step
    width = heads_per_step * HEAD_DIM
    return pl.pallas_call(
        functools.partial(_moba_gate_kernel, tq=tq),
        grid=(groups, s // tq),
        in_specs=[pl.BlockSpec((tq, width), lambda h, i: (i, h)),
                  pl.BlockSpec((s, width), lambda h, i: (0, groups + h))],
        out_specs=pl.BlockSpec((heads_per_step, nb, tq), lambda h, i: (h, 0, i)),
        out_shape=jax.ShapeDtypeStruct((ATTN_HEADS, nb, s), F32),
        compiler_params=_params("parallel", "arbitrary"),
        name="moba_gate",
    )(qk, qk)


SCORE_GROUP = 8
PROB_GROUP = 4
HEADS_PER_STEP = 4
LOG2E = 1.4426950408889634


def _moba_attn_kernel(zslope_ref, q_ref, k_ref, kb_ref, vt_ref, bias_ref, o_ref, s_ref):
    qb = pl.program_id(1)
    bs = MOBA_BLOCK
    hd = HEAD_DIM
    c = (hd ** -0.5) * LOG2E
    heads = range(HEADS_PER_STEP)
    zslope = [zslope_ref[pl.program_id(0) * HEADS_PER_STEP + a] for a in heads]
    nt = (((1,), (1,)), ((), ()))
    ones3 = (lax.broadcasted_iota(jnp.int32, (bs, hd), 1) < 3).astype(BF16)
    q2 = [jnp.concatenate([q_ref[:, a * hd:(a + 1) * hd], ones3], axis=1) for a in heads]

    def scores(a, row0, n_rows):
        rows = pl.ds(pl.multiple_of(row0, bs), n_rows)
        k2 = jnp.concatenate([k_ref[rows, a * hd:(a + 1) * hd], kb_ref[a, 0:n_rows, :]], axis=1)
        return lax.dot_general(k2, q2[a], nt, preferred_element_type=F32)

    def pick(a, kb):
        return zslope[a] * (kb - qb).astype(F32) + bias_ref[a, pl.ds(kb, 1), :]

    causal = lax.broadcasted_iota(jnp.int32, (bs, bs), 0) <= lax.broadcasted_iota(jnp.int32, (bs, bs), 1)
    z_own = [jnp.where(causal, c * scores(a, qb * bs, bs), NEG_INF) for a in heads]
    m = tuple(jnp.max(z_own[a], axis=0, keepdims=True) for a in heads)

    def span_loop(span, group, carry):
        n_full = qb // group
        rest = qb - n_full * group
        carry = lax.fori_loop(0, n_full, lambda g, cr: span(g * group, group, cr), carry)
        return lax.cond(
            rest > group // 2,
            lambda cr: span(n_full * group, group, cr),
            lambda cr: lax.cond(rest > 0, lambda c2: span(n_full * group, group // 2, c2), lambda c2: c2, cr),
            carry)

    def pass1(kb0, n_blocks, m):
        m = list(m)
        for a in heads:
            z = c * scores(a, kb0 * bs, n_blocks * bs)
            s_ref[a, pl.ds(pl.multiple_of(kb0 * bs, bs), n_blocks * bs), :] = z
            for b in range(n_blocks):
                top = jnp.max(z[b * bs:(b + 1) * bs], axis=0, keepdims=True)
                m[a] = jnp.maximum(m[a], top + pick(a, kb0 + b))
        return tuple(m)

    m = span_loop(pass1, SCORE_GROUP, m)

    acc = tuple(_dot(vt_ref[a, qb], jnp.exp2(z_own[a] - m[a]).astype(BF16)) for a in heads)

    def pass2(kb0, n_blocks, acc):
        acc = list(acc)
        for b in range(n_blocks):
            kb = kb0 + b
            for a in heads:
                z = s_ref[a, pl.ds(pl.multiple_of(kb * bs, bs), bs), :]
                w = pick(a, kb) - m[a]
                acc[a] = acc[a] + _dot(vt_ref[a, kb], jnp.exp2(z + w).astype(BF16))
        return tuple(acc)

    acc = span_loop(pass2, PROB_GROUP, acc)
    for a in heads:
        o_ref[:, a * hd:(a + 1) * hd] = (acc[a][0:hd] / acc[a][hd:hd + 1]).T.astype(o_ref.dtype)


def _moba_attention(qk, vt, bias):
    s = qk.shape[0]
    nb = s // MOBA_BLOCK
    assert nb % SCORE_GROUP == 0 and SCORE_GROUP % PROB_GROUP == 0
    scale = HEAD_DIM ** -0.5
    slopes = np.exp2(-8.0 * np.arange(1, ATTN_HEADS + 1, dtype=np.float32) / ATTN_HEADS).astype(np.float32)
    zslope = jnp.asarray(slopes * np.float32(LOG2E * MOBA_BLOCK))
    rest = (slopes[:, None] * np.arange(MOBA_BLOCK, dtype=np.float32)[None, :] / np.float32(scale)).astype(np.float32)
    cols = []
    for _ in range(3):
        part = rest.astype(BF16)
        cols.append(part)
        rest = rest - part.astype(np.float32)
    kbias = np.zeros((ATTN_HEADS, MOBA_BLOCK, HEAD_DIM), BF16)
    kbias[:, :, 0:3] = np.stack(cols, axis=-1)
    kbias = jnp.asarray(np.tile(kbias, (1, SCORE_GROUP, 1)))
    hps = HEADS_PER_STEP
    groups = ATTN_HEADS // hps
    once = pl.Buffered(1)
    return pl.pallas_call(
        _moba_attn_kernel,
        grid=(groups, nb),
        in_specs=[pl.BlockSpec(memory_space=pltpu.SMEM),
                  pl.BlockSpec((MOBA_BLOCK, hps * HEAD_DIM), lambda h, i: (i, h)),
                  pl.BlockSpec((s, hps * HEAD_DIM), lambda h, i: (0, groups + h), pipeline_mode=once),
                  pl.BlockSpec((hps, SCORE_GROUP * MOBA_BLOCK, HEAD_DIM), lambda h, i: (h, 0, 0), pipeline_mode=once),
                  pl.BlockSpec((hps, nb, HEAD_DIM + ONES_ROWS, MOBA_BLOCK), lambda h, i: (h, 0, 0, 0),
                               pipeline_mode=once),
                  pl.BlockSpec((hps, nb, MOBA_BLOCK), lambda h, i: (h, 0, i))],
        out_specs=pl.BlockSpec((MOBA_BLOCK, hps * HEAD_DIM), lambda h, i: (i, h)),
        out_shape=jax.ShapeDtypeStruct((s, ATTN_HEADS * HEAD_DIM), BF16),
        scratch_shapes=[pltpu.VMEM((hps, s, MOBA_BLOCK), F32)],
        compiler_params=_params("parallel", "arbitrary"),
        name="moba_attention",
    )(zslope, qk, qk, kbias, vt, bias)


def _conv_ln_silu_kernel(x_ref, halo_ref, w_ref, b_ref, g_ref, beta_ref, o_ref, win_ref, y_ref, *, taps):
    i = pl.program_id(0)
    j = pl.program_id(1)
    nj = y_ref.shape[0]
    t, ct = x_ref.shape
    hr = halo_ref.shape[0]
    win_ref[0, 0:hr, :] = jnp.where(i == 0, 0.0, halo_ref[...])
    win_ref[0, hr:, :] = x_ref[...]
    span = hr + t - F32_SUBLANES
    for p in range(1, F32_SUBLANES):
        win_ref[p, 0:span, :] = win_ref[0, pl.ds(p, span), :]
    for c in range(ct // LANES):
        lanes = slice(c * LANES, (c + 1) * LANES)
        acc = jnp.broadcast_to(b_ref[:, lanes], (t, LANES))
        for k in range(taps):
            a, p = divmod(hr - (taps - 1) + k, F32_SUBLANES)
            acc = acc + w_ref[k:k + 1, lanes] * win_ref[p, F32_SUBLANES * a:F32_SUBLANES * a + t, lanes]
        y_ref[j, :, lanes] = acc

    @pl.when(j == nj - 1)
    def _():
        n = nj * ct
        mu = sum(jnp.sum(y_ref[jj], axis=-1, keepdims=True) for jj in range(nj)) * (1.0 / n)
        var = sum(jnp.sum(jnp.square(y_ref[jj] - mu), axis=-1, keepdims=True) for jj in range(nj)) * (1.0 / n)
        inv = lax.rsqrt(var + LN_EPS)
        for jj in range(nj):
            cols = slice(jj * ct, (jj + 1) * ct)
            y = (y_ref[jj] - mu) * inv * g_ref[:, cols] + beta_ref[:, cols]
            o_ref[:, cols] = (y * _sigmoid(y)).astype(o_ref.dtype)


def _conv_ln_silu(x, w, b, g, beta, rows=256, cols=512):
    s, c = x.shape
    taps = w.shape[0]
    assert taps - 1 <= CONV_HALO_ROWS
    per = rows // CONV_HALO_ROWS
    nj = c // cols
    return pl.pallas_call(
        functools.partial(_conv_ln_silu_kernel, taps=taps),
        grid=(s // rows, nj),
        in_specs=[pl.BlockSpec((rows, cols), lambda i, j: (i, j)),
                  pl.BlockSpec((CONV_HALO_ROWS, cols), lambda i, j: (jnp.maximum(i * per - 1, 0), j)),
                  pl.BlockSpec((taps, cols), lambda i, j: (0, j)),
                  pl.BlockSpec((1, cols), lambda i, j: (0, j)),
                  pl.BlockSpec((1, c), lambda i, j: (0, 0)),
                  pl.BlockSpec((1, c), lambda i, j: (0, 0))],
        out_specs=pl.BlockSpec((rows, c), lambda i, j: (i, 0)),
        out_shape=jax.ShapeDtypeStruct((s, c), BF16),
        scratch_shapes=[pltpu.VMEM((F32_SUBLANES, rows + CONV_HALO_ROWS, cols), F32),
                        pltpu.VMEM((nj, rows, cols), F32)],
        compiler_params=_params("parallel", "arbitrary"),
        name="conformer_conv_ln_silu",
    )(x, x, w, b.reshape(1, c), g.reshape(1, c), beta.reshape(1, c))


def _merge_kernel(u_ref, a_ref, c_ref, wga_ref, wgc_ref, wpa_ref, wpc_ref, o_ref, bga_ref, bgc_ref, bpa_ref, bpc_ref):
    _cast_weights_once([wga_ref, wgc_ref, wpa_ref, wpc_ref], [bga_ref, bgc_ref, bpa_ref, bpc_ref])
    u = u_ref[...]
    ga = _sigmoid(_dot(u, bga_ref[...]))
    gc = _sigmoid(_dot(u, bgc_ref[...]))
    o_ref[...] = (ga * _dot(a_ref[...], bpa_ref[...]) + gc * _dot(c_ref[...], bpc_ref[...])).astype(o_ref.dtype)


def _merge(u, attn, cact, w_in, col_ga, col_gc, w_pa, w_pc, tm, tn):
    m, d = u.shape
    n = w_pa.shape[1]
    oa, oc = col_ga // tn, col_gc // tn
    return pl.pallas_call(
        _merge_kernel,
        grid=(n // tn, m // tm),
        in_specs=[pl.BlockSpec((tm, d), lambda j, i: (i, 0)),
                  pl.BlockSpec((tm, attn.shape[1]), lambda j, i: (i, 0)),
                  pl.BlockSpec((tm, cact.shape[1]), lambda j, i: (i, 0)),
                  pl.BlockSpec((d, tn), lambda j, i: (0, j + oa)),
                  pl.BlockSpec((d, tn), lambda j, i: (0, j + oc)),
                  pl.BlockSpec((w_pa.shape[0], tn), lambda j, i: (0, j)),
                  pl.BlockSpec((w_pc.shape[0], tn), lambda j, i: (0, j))],
        out_specs=pl.BlockSpec((tm, tn), lambda j, i: (i, j)),
        out_shape=jax.ShapeDtypeStruct((m, n), BF16),
        scratch_shapes=[pltpu.VMEM((d, tn), BF16), pltpu.VMEM((d, tn), BF16),
                        pltpu.VMEM((w_pa.shape[0], tn), BF16), pltpu.VMEM((w_pc.shape[0], tn), BF16)],
        compiler_params=_params("parallel", "arbitrary"),
        name="gated_merge",
    )(u, attn, cact, w_in, w_in, w_pa, w_pc)


def _ffn_up_kernel(f_ref, fh_ref, wg_ref, wv_ref, cwg_ref, cwv_ref, cbg_ref, cbv_ref, o_ref, win_ref, bg_ref, bv_ref):
    _cast_weights_once([wg_ref, wv_ref], [bg_ref, bv_ref])
    tm = f_ref.shape[0]
    hr = fh_ref.shape[0]
    taps = cwg_ref.shape[0]
    keep = (pl.program_id(1) > 0).astype(F32)
    f = f_ref[...]
    fh = fh_ref[...]

    def conv(z, w_ref, cw_ref, cb_ref):
        up = _dot(f, w_ref[...])
        win_ref[z, 0:hr, :] = _dot(fh, w_ref[...]) * keep
        win_ref[z, hr:, :] = up
        y = cw_ref[taps - 1:taps, :] * up + cb_ref[...]
        for k in range(taps - 1):
            y = y + cw_ref[k:k + 1, :] * win_ref[z, pl.ds(hr - (taps - 1) + k, tm), :]
        return y

    g = conv(0, bg_ref, cwg_ref, cbg_ref)
    act = g * _sigmoid(g)
    v = conv(1, bv_ref, cwv_ref, cbv_ref)
    o_ref[...] = (act * v).astype(o_ref.dtype)


def _ffn_up(f, w_up, w_dw, b_dw, d_ff, tm, tn):
    m, d = f.shape
    taps = w_dw.shape[0]
    off = d_ff // tn
    per = tm // BF16_SUBLANES
    return pl.pallas_call(
        _ffn_up_kernel,
        grid=(d_ff // tn, m // tm),
        in_specs=[pl.BlockSpec((tm, d), lambda j, i: (i, 0)),
                  pl.BlockSpec((BF16_SUBLANES, d), lambda j, i: (jnp.maximum(i * per - 1, 0), 0)),
                  pl.BlockSpec((d, tn), lambda j, i: (0, j)),
                  pl.BlockSpec((d, tn), lambda j, i: (0, j + off)),
                  pl.BlockSpec((taps, tn), lambda j, i: (0, j)),
                  pl.BlockSpec((taps, tn), lambda j, i: (0, j + off)),
                  pl.BlockSpec((1, tn), lambda j, i: (0, j)),
                  pl.BlockSpec((1, tn), lambda j, i: (0, j + off))],
        out_specs=pl.BlockSpec((tm, tn), lambda j, i: (i, j)),
        out_shape=jax.ShapeDtypeStruct((m, d_ff), BF16),
        scratch_shapes=[pltpu.VMEM((2, tm + BF16_SUBLANES, tn), F32),
                        pltpu.VMEM((d, tn), BF16), pltpu.VMEM((d, tn), BF16)],
        compiler_params=_params("parallel", "arbitrary"),
        name="convffn_up",
    )(f, f, w_up, w_up, w_dw, w_dw, b_dw.reshape(1, -1), b_dw.reshape(1, -1))


def kernel(x, g_mix, w_in, w_conv_dw, b_conv_dw, ln_conv_g, ln_conv_b, w_proj_attn, w_proj_conv, w_out,
           g_ffn, w_up, w_ffn_dw, b_ffn_dw, w_down, g_final):
    b, s, d = x.shape
    assert b == 1, "single-sequence layer"
    attn_w = ATTN_HEADS * HEAD_DIM
    conv_c = w_conv_dw.shape[1]
    d_ff = w_down.shape[0]
    col_glu_a = 3 * attn_w
    col_glu_b = col_glu_a + conv_c
    col_ga = col_glu_b + conv_c
    col_gc = col_ga + d

    x2 = x.reshape(s, d)
    u = _rmsnorm(x2, g_mix, BF16, "rmsnorm_mix")
    qk = _matmul(u, w_in, 2 * attn_w, 0, *TILES["qk_proj"], BF16, "qk_proj")
    vt = _v_proj(u, w_in, 2 * attn_w, *TILES["v_proj"])
    c0 = _glu(u, w_in, col_glu_a, col_glu_b, conv_c, *TILES["glu_proj"])

    bias = _moba_gate(qk)
    attn = _moba_attention(qk, vt, bias)

    cact = _conv_ln_silu(c0, w_conv_dw, b_conv_dw, ln_conv_g, ln_conv_b)

    merged = _merge(u, attn, cact, w_in, col_ga, col_gc, w_proj_attn, w_proj_conv, *TILES["gated_merge"])
    h = _matmul(merged, w_out, d, 0, *TILES["out_proj"], F32, "out_proj", residual=x2)

    f = _rmsnorm(h, g_ffn, BF16, "rmsnorm_ffn")
    g = _ffn_up(f, w_up, w_ffn_dw, b_ffn_dw, d_ff, *TILES["convffn_up"])
    h2 = _matmul_rows_resident(g, w_down.astype(BF16), *TILES["ffn_down"], "ffn_down", residual=h)
    out = _rmsnorm(h2, g_final, F32, "rmsnorm_final")
    return out.reshape(b, s, d)
```

```python
import functools

import jax
import jax.numpy as jnp
import numpy as np
from jax import lax
from jax.experimental import pallas as pl
from jax.experimental.pallas import tpu as pltpu

ATTN_HEADS = 16
HEAD_DIM = 128
MOBA_BLOCK = 256
MOBA_TOPK = 3
NORM_EPS = 1e-6
LN_EPS = 1e-5
NEG_INF = -1e30

F32 = jnp.float32
BF16 = jnp.bfloat16

V7X_VMEM_BYTES = 64 * 1024 * 1024
VMEM_LIMIT_BYTES = V7X_VMEM_BYTES - 8 * 1024 * 1024
LANES = 128
F32_SUBLANES = 8
BF16_SUBLANES = 16
CONV_HALO_ROWS = 32
ONES_ROWS = 16

TILES = {
    "qk_proj": (512, 1024),
    "v_proj": (512, 1024),
    "glu_proj": (512, 512),
    "gated_merge": (512, 256),
    "out_proj": (1024, 512),
    "convffn_up": (1024, 256),
    "ffn_down": (512, 512),
}


def _params(*sem):
    return pltpu.CompilerParams(dimension_semantics=sem, vmem_limit_bytes=VMEM_LIMIT_BYTES)


def _dot(a, b):
    return jnp.dot(a, b, preferred_element_type=F32)


def _sigmoid(x):
    return 0.5 * jnp.tanh(0.5 * x) + 0.5


def _cast_weights_once(w_refs, wb_refs):
    @pl.when(pl.program_id(1) == 0)
    def _():
        for w_ref, wb_ref in zip(w_refs, wb_refs):
            wb_ref[...] = w_ref[...].astype(BF16)


def _rmsnorm_kernel(x_ref, g_ref, o_ref):
    x = x_ref[...]
    ms = jnp.mean(x * x, axis=-1, keepdims=True)
    o_ref[...] = (x * lax.rsqrt(ms + NORM_EPS) * g_ref[...]).astype(o_ref.dtype)


def _rmsnorm(x, g, out_dtype, name, rows=512):
    s, d = x.shape
    return pl.pallas_call(
        _rmsnorm_kernel,
        grid=(s // rows,),
        in_specs=[pl.BlockSpec((rows, d), lambda i: (i, 0)),
                  pl.BlockSpec((1, d), lambda i: (0, 0))],
        out_specs=pl.BlockSpec((rows, d), lambda i: (i, 0)),
        out_shape=jax.ShapeDtypeStruct((s, d), out_dtype),
        compiler_params=_params("parallel"),
        name=name,
    )(x, g.reshape(1, d))


def _mm_kernel(a_ref, w_ref, o_ref, wb_ref):
    _cast_weights_once([w_ref], [wb_ref])
    o_ref[...] = _dot(a_ref[...], wb_ref[...]).astype(o_ref.dtype)


def _mm_residual_kernel(a_ref, w_ref, r_ref, o_ref, wb_ref):
    _cast_weights_once([w_ref], [wb_ref])
    o_ref[...] = r_ref[...] + _dot(a_ref[...], wb_ref[...])


def _matmul(a, w, n, col0, tm, tn, out_dtype, name, residual=None):
    m, k = a.shape
    off = col0 // tn
    in_specs = [pl.BlockSpec((tm, k), lambda j, i: (i, 0)),
                pl.BlockSpec((k, tn), lambda j, i: (0, j + off))]
    args = [a, w]
    kern = _mm_kernel
    if residual is not None:
        in_specs.append(pl.BlockSpec((tm, tn), lambda j, i: (i, j)))
        args.append(residual)
        kern = _mm_residual_kernel
    return pl.pallas_call(
        kern,
        grid=(n // tn, m // tm),
        in_specs=in_specs,
        out_specs=pl.BlockSpec((tm, tn), lambda j, i: (i, j)),
        out_shape=jax.ShapeDtypeStruct((m, n), out_dtype),
        scratch_shapes=[pltpu.VMEM((k, tn), BF16)],
        compiler_params=_params("parallel", "arbitrary"),
        name=name,
    )(*args)


def _v_proj_kernel(a_ref, w_ref, o_ref, wb_ref):
    _cast_weights_once([w_ref], [wb_ref])
    acc = _dot(a_ref[...], wb_ref[...])
    n_heads, n_blocks = o_ref.shape[0], o_ref.shape[1]
    ones = jnp.ones((ONES_ROWS, MOBA_BLOCK), o_ref.dtype)
    for hh in range(n_heads):
        for rb in range(n_blocks):
            blk = acc[rb * MOBA_BLOCK:(rb + 1) * MOBA_BLOCK, hh * HEAD_DIM:(hh + 1) * HEAD_DIM]
            o_ref[hh, rb, 0:HEAD_DIM, :] = blk.T.astype(o_ref.dtype)
            o_ref[hh, rb, HEAD_DIM:, :] = ones


def _v_proj(a, w, col0, tm, tn):
    m, k = a.shape
    off = col0 // tn
    nb = m // MOBA_BLOCK
    return pl.pallas_call(
        _v_proj_kernel,
        grid=(ATTN_HEADS * HEAD_DIM // tn, m // tm),
        in_specs=[pl.BlockSpec((tm, k), lambda j, i: (i, 0)),
                  pl.BlockSpec((k, tn), lambda j, i: (0, j + off))],
        out_specs=pl.BlockSpec((tn // HEAD_DIM, tm // MOBA_BLOCK, HEAD_DIM + ONES_ROWS, MOBA_BLOCK),
                               lambda j, i: (j, i, 0, 0)),
        out_shape=jax.ShapeDtypeStruct((ATTN_HEADS, nb, HEAD_DIM + ONES_ROWS, MOBA_BLOCK), BF16),
        scratch_shapes=[pltpu.VMEM((k, tn), BF16)],
        compiler_params=_params("parallel", "arbitrary"),
        name="v_proj",
    )(a, w)


def _mm_bf16_residual_kernel(a_ref, w_ref, r_ref, o_ref):
    o_ref[...] = r_ref[...] + _dot(a_ref[...], w_ref[...])


def _matmul_rows_resident(a, w, tm, tn, name, residual):
    m, k = a.shape
    n = w.shape[1]
    return pl.pallas_call(
        _mm_bf16_residual_kernel,
        grid=(m // tm, n // tn),
        in_specs=[pl.BlockSpec((tm, k), lambda i, j: (i, 0)),
                  pl.BlockSpec((k, tn), lambda i, j: (0, j)),
                  pl.BlockSpec((tm, tn), lambda i, j: (i, j))],
        out_specs=pl.BlockSpec((tm, tn), lambda i, j: (i, j)),
        out_shape=jax.ShapeDtypeStruct((m, n), F32),
        compiler_params=_params("parallel", "arbitrary"),
        name=name,
    )(a, w, residual)


def _glu_kernel(u_ref, wa_ref, wb_ref, o_ref, wab_ref, wbb_ref):
    _cast_weights_once([wa_ref, wb_ref], [wab_ref, wbb_ref])
    u = u_ref[...]
    o_ref[...] = _dot(u, wab_ref[...]) * _sigmoid(_dot(u, wbb_ref[...]))


def _glu(u, w_in, col_a, col_b, n, tm, tn):
    m, k = u.shape
    oa, ob = col_a // tn, col_b // tn
    return pl.pallas_call(
        _glu_kernel,
        grid=(n // tn, m // tm),
        in_specs=[pl.BlockSpec((tm, k), lambda j, i: (i, 0)),
                  pl.BlockSpec((k, tn), lambda j, i: (0, j + oa)),
                  pl.BlockSpec((k, tn), lambda j, i: (0, j + ob))],
        out_specs=pl.BlockSpec((tm, tn), lambda j, i: (i, j)),
        out_shape=jax.ShapeDtypeStruct((m, n), F32),
        scratch_shapes=[pltpu.VMEM((k, tn), BF16), pltpu.VMEM((k, tn), BF16)],
        compiler_params=_params("parallel", "arbitrary"),
        name="glu_proj",
    )(u, w_in, w_in)


def _moba_gate_kernel(q_ref, k_ref, o_ref, *, tq):
    s = k_ref.shape[0]
    nb = s // MOBA_BLOCK
    blk = lax.broadcasted_iota(jnp.int32, (nb, tq), 0)
    qpos = pl.program_id(1) * tq + lax.broadcasted_iota(jnp.int32, (nb, tq), 1)
    past = blk < qpos // MOBA_BLOCK
    for a in range(o_ref.shape[0]):
        cols = slice(a * HEAD_DIM, (a + 1) * HEAD_DIM)
        k = k_ref[:, cols].astype(F32).reshape(nb, MOBA_BLOCK, HEAD_DIM)
        k_mean = jnp.sum(k, axis=1) * (1.0 / MOBA_BLOCK)
        q = q_ref[:, cols]
        gate = jnp.zeros((nb, tq), F32)
        rest = k_mean
        for _ in range(3):
            part = rest.astype(BF16)
            gate = gate + lax.dot_general(part, q, (((1,), (1,)), ((), ())), preferred_element_type=F32)
            rest = rest - part.astype(F32)
        gate = jnp.where(past, gate, NEG_INF)
        picked = jnp.zeros((nb, tq), jnp.bool_)
        for _ in range(min(MOBA_TOPK, nb)):
            top = jnp.max(gate, axis=0, keepdims=True)
            first = jnp.min(jnp.where(gate == top, blk, nb), axis=0, keepdims=True)
            hit = blk == first
            picked = jnp.logical_or(picked, hit)
            gate = jnp.where(hit, -jnp.inf, gate)
        o_ref[a] = jnp.where(jnp.logical_and(picked, past), 0.0, NEG_INF)


def _moba_gate(qk, tq=4096, heads_per_step=4):
    s = qk.shape[0]
    nb = s // MOBA_BLOCK
    groups = ATTN_HEADS // heads_per_step
    width = heads_per_step * HEAD_DIM
    return pl.pallas_call(
        functools.partial(_moba_gate_kernel, tq=tq),
        grid=(groups, s // tq),
        in_specs=[pl.BlockSpec((tq, width), lambda h, i: (i, h)),
                  pl.BlockSpec((s, width), lambda h, i: (0, groups + h))],
        out_specs=pl.BlockSpec((heads_per_step, nb, tq), lambda h, i: (h, 0, i)),
        out_shape=jax.ShapeDtypeStruct((ATTN_HEADS, nb, s), F32),
        compiler_params=_params("parallel", "arbitrary"),
        name="moba_gate",
    )(qk, qk)


SCORE_GROUP = 8
PROB_GROUP = 4
HEADS_PER_STEP = 4
LOG2E = 1.4426950408889634


def _moba_attn_kernel(zslope_ref, q_ref, k_ref, kb_ref, vt_ref, bias_ref, o_ref, s_ref):
    qb = pl.program_id(1)
    bs = MOBA_BLOCK
    hd = HEAD_DIM
    c = (hd ** -0.5) * LOG2E
    heads = range(HEADS_PER_STEP)
    zslope = [zslope_ref[pl.program_id(0) * HEADS_PER_STEP + a] for a in heads]
    nt = (((1,), (1,)), ((), ()))
    ones3 = (lax.broadcasted_iota(jnp.int32, (bs, hd), 1) < 3).astype(BF16)
    q2 = [jnp.concatenate([q_ref[:, a * hd:(a + 1) * hd], ones3], axis=1) for a in heads]

    def scores(a, row0, n_rows):
        rows = pl.ds(pl.multiple_of(row0, bs), n_rows)
        k2 = jnp.concatenate([k_ref[rows, a * hd:(a + 1) * hd], kb_ref[a, 0:n_rows, :]], axis=1)
        return lax.dot_general(k2, q2[a], nt, preferred_element_type=F32)

    def pick(a, kb):
        return zslope[a] * (kb - qb).astype(F32) + bias_ref[a, pl.ds(kb, 1), :]

    causal = lax.broadcasted_iota(jnp.int32, (bs, bs), 0) <= lax.broadcasted_iota(jnp.int32, (bs, bs), 1)
    z_own = [jnp.where(causal, c * scores(a, qb * bs, bs), NEG_INF) for a in heads]
    m = tuple(jnp.max(z_own[a], axis=0, keepdims=True) for a in heads)

    def span_loop(span, group, carry):
        n_full = qb // group
        rest = qb - n_full * group
        carry = lax.fori_loop(0, n_full, lambda g, cr: span(g * group, group, cr), carry)
        return lax.cond(
            rest > group // 2,
            lambda cr: span(n_full * group, group, cr),
            lambda cr: lax.cond(rest > 0, lambda c2: span(n_full * group, group // 2, c2), lambda c2: c2, cr),
            carry)

    def pass1(kb0, n_blocks, m):
        m = list(m)
        for a in heads:
            z = c * scores(a, kb0 * bs, n_blocks * bs)
            s_ref[a, pl.ds(pl.multiple_of(kb0 * bs, bs), n_blocks * bs), :] = z
            for b in range(n_blocks):
                top = jnp.max(z[b * bs:(b + 1) * bs], axis=0, keepdims=True)
                m[a] = jnp.maximum(m[a], top + pick(a, kb0 + b))
        return tuple(m)

    m = span_loop(pass1, SCORE_GROUP, m)

    acc = tuple(_dot(vt_ref[a, qb], jnp.exp2(z_own[a] - m[a]).astype(BF16)) for a in heads)

    def pass2(kb0, n_blocks, acc):
        acc = list(acc)
        for b in range(n_blocks):
            kb = kb0 + b
            for a in heads:
                z = s_ref[a, pl.ds(pl.multiple_of(kb * bs, bs), bs), :]
                w = pick(a, kb) - m[a]
                acc[a] = acc[a] + _dot(vt_ref[a, kb], jnp.exp2(z + w).astype(BF16))
        return tuple(acc)

    acc = span_loop(pass2, PROB_GROUP, acc)
    for a in heads:
        o_ref[:, a * hd:(a + 1) * hd] = (acc[a][0:hd] / acc[a][hd:hd + 1]).T.astype(o_ref.dtype)


def _moba_attention(qk, vt, bias):
    s = qk.shape[0]
    nb = s // MOBA_BLOCK
    assert nb % SCORE_GROUP == 0 and SCORE_GROUP % PROB_GROUP == 0
    scale = HEAD_DIM ** -0.5
    slopes = np.exp2(-8.0 * np.arange(1, ATTN_HEADS + 1, dtype=np.float32) / ATTN_HEADS).astype(np.float32)
    zslope = jnp.asarray(slopes * np.float32(LOG2E * MOBA_BLOCK))
    rest = (slopes[:, None] * np.arange(MOBA_BLOCK, dtype=np.float32)[None, :] / np.float32(scale)).astype(np.float32)
    cols = []
    for _ in range(3):
        part = rest.astype(BF16)
        cols.append(part)
        rest = rest - part.astype(np.float32)
    kbias = np.zeros((ATTN_HEADS, MOBA_BLOCK, HEAD_DIM), BF16)
    kbias[:, :, 0:3] = np.stack(cols, axis=-1)
    kbias = jnp.asarray(np.tile(kbias, (1, SCORE_GROUP, 1)))
    hps = HEADS_PER_STEP
    groups = ATTN_HEADS // hps
    once = pl.Buffered(1)
    return pl.pallas_call(
        _moba_attn_kernel,
        grid=(groups, nb),
        in_specs=[pl.BlockSpec(memory_space=pltpu.SMEM),
                  pl.BlockSpec((MOBA_BLOCK, hps * HEAD_DIM), lambda h, i: (i, h)),
                  pl.BlockSpec((s, hps * HEAD_DIM), lambda h, i: (0, groups + h), pipeline_mode=once),
                  pl.BlockSpec((hps, SCORE_GROUP * MOBA_BLOCK, HEAD_DIM), lambda h, i: (h, 0, 0), pipeline_mode=once),
                  pl.BlockSpec((hps, nb, HEAD_DIM + ONES_ROWS, MOBA_BLOCK), lambda h, i: (h, 0, 0, 0),
                               pipeline_mode=once),
                  pl.BlockSpec((hps, nb, MOBA_BLOCK), lambda h, i: (h, 0, i))],
        out_specs=pl.BlockSpec((MOBA_BLOCK, hps * HEAD_DIM), lambda h, i: (i, h)),
        out_shape=jax.ShapeDtypeStruct((s, ATTN_HEADS * HEAD_DIM), BF16),
        scratch_shapes=[pltpu.VMEM((hps, s, MOBA_BLOCK), F32)],
        compiler_params=_params("parallel", "arbitrary"),
        name="moba_attention",
    )(zslope, qk, qk, kbias, vt, bias)


def _conv_ln_silu_kernel(x_ref, halo_ref, w_ref, b_ref, g_ref, beta_ref, o_ref, win_ref, y_ref, *, taps):
    i = pl.program_id(0)
    j = pl.program_id(1)
    nj = y_ref.shape[0]
    t, ct = x_ref.shape
    hr = halo_ref.shape[0]
    win_ref[0, 0:hr, :] = jnp.where(i == 0, 0.0, halo_ref[...])
    win_ref[0, hr:, :] = x_ref[...]
    span = hr + t - F32_SUBLANES
    for p in range(1, F32_SUBLANES):
        win_ref[p, 0:span, :] = win_ref[0, pl.ds(p, span), :]
    for c in range(ct // LANES):
        lanes = slice(c * LANES, (c + 1) * LANES)
        acc = jnp.broadcast_to(b_ref[:, lanes], (t, LANES))
        for k in range(taps):
            a, p = divmod(hr - (taps - 1) + k, F32_SUBLANES)
            acc = acc + w_ref[k:k + 1, lanes] * win_ref[p, F32_SUBLANES * a:F32_SUBLANES * a + t, lanes]
        y_ref[j, :, lanes] = acc

    @pl.when(j == nj - 1)
    def _():
        n = nj * ct
        mu = sum(jnp.sum(y_ref[jj], axis=-1, keepdims=True) for jj in range(nj)) * (1.0 / n)
        var = sum(jnp.sum(jnp.square(y_ref[jj] - mu), axis=-1, keepdims=True) for jj in range(nj)) * (1.0 / n)
        inv = lax.rsqrt(var + LN_EPS)
        for jj in range(nj):
            cols = slice(jj * ct, (jj + 1) * ct)
            y = (y_ref[jj] - mu) * inv * g_ref[:, cols] + beta_ref[:, cols]
            o_ref[:, cols] = (y * _sigmoid(y)).astype(o_ref.dtype)


def _conv_ln_silu(x, w, b, g, beta, rows=256, cols=512):
    s, c = x.shape
    taps = w.shape[0]
    assert taps - 1 <= CONV_HALO_ROWS
    per = rows // CONV_HALO_ROWS
    nj = c // cols
    return pl.pallas_call(
        functools.partial(_conv_ln_silu_kernel, taps=taps),
        grid=(s // rows, nj),
        in_specs=[pl.BlockSpec((rows, cols), lambda i, j: (i, j)),
                  pl.BlockSpec((CONV_HALO_ROWS, cols), lambda i, j: (jnp.maximum(i * per - 1, 0), j)),
                  pl.BlockSpec((taps, cols), lambda i, j: (0, j)),
                  pl.BlockSpec((1, cols), lambda i, j: (0, j)),
                  pl.BlockSpec((1, c), lambda i, j: (0, 0)),
                  pl.BlockSpec((1, c), lambda i, j: (0, 0))],
        out_specs=pl.BlockSpec((rows, c), lambda i, j: (i, 0)),
        out_shape=jax.ShapeDtypeStruct((s, c), BF16),
        scratch_shapes=[pltpu.VMEM((F32_SUBLANES, rows + CONV_HALO_ROWS, cols), F32),
                        pltpu.VMEM((nj, rows, cols), F32)],
        compiler_params=_params("parallel", "arbitrary"),
        name="conformer_conv_ln_silu",
    )(x, x, w, b.reshape(1, c), g.reshape(1, c), beta.reshape(1, c))


def _merge_kernel(u_ref, a_ref, c_ref, wga_ref, wgc_ref, wpa_ref, wpc_ref, o_ref, bga_ref, bgc_ref, bpa_ref, bpc_ref):
    _cast_weights_once([wga_ref, wgc_ref, wpa_ref, wpc_ref], [bga_ref, bgc_ref, bpa_ref, bpc_ref])
    u = u_ref[...]
    ga = _sigmoid(_dot(u, bga_ref[...]))
    gc = _sigmoid(_dot(u, bgc_ref[...]))
    o_ref[...] = (ga * _dot(a_ref[...], bpa_ref[...]) + gc * _dot(c_ref[...], bpc_ref[...])).astype(o_ref.dtype)


def _merge(u, attn, cact, w_in, col_ga, col_gc, w_pa, w_pc, tm, tn):
    m, d = u.shape
    n = w_pa.shape[1]
    oa, oc = col_ga // tn, col_gc // tn
    return pl.pallas_call(
        _merge_kernel,
        grid=(n // tn, m // tm),
        in_specs=[pl.BlockSpec((tm, d), lambda j, i: (i, 0)),
                  pl.BlockSpec((tm, attn.shape[1]), lambda j, i: (i, 0)),
                  pl.BlockSpec((tm, cact.shape[1]), lambda j, i: (i, 0)),
                  pl.BlockSpec((d, tn), lambda j, i: (0, j + oa)),
                  pl.BlockSpec((d, tn), lambda j, i: (0, j + oc)),
                  pl.BlockSpec((w_pa.shape[0], tn), lambda j, i: (0, j)),
                  pl.BlockSpec((w_pc.shape[0], tn), lambda j, i: (0, j))],
        out_specs=pl.BlockSpec((tm, tn), lambda j, i: (i, j)),
        out_shape=jax.ShapeDtypeStruct((m, n), BF16),
        scratch_shapes=[pltpu.VMEM((d, tn), BF16), pltpu.VMEM((d, tn), BF16),
                        pltpu.VMEM((w_pa.shape[0], tn), BF16), pltpu.VMEM((w_pc.shape[0], tn), BF16)],
        compiler_params=_params("parallel", "arbitrary"),
        name="gated_merge",
    )(u, attn, cact, w_in, w_in, w_pa, w_pc)


def _ffn_up_kernel(f_ref, fh_ref, wg_ref, wv_ref, cwg_ref, cwv_ref, cbg_ref, cbv_ref, o_ref, win_ref, bg_ref, bv_ref):
    _cast_weights_once([wg_ref, wv_ref], [bg_ref, bv_ref])
    tm = f_ref.shape[0]
    hr = fh_ref.shape[0]
    taps = cwg_ref.shape[0]
    keep = (pl.program_id(1) > 0).astype(F32)
    f = f_ref[...]
    fh = fh_ref[...]

    def conv(z, w_ref, cw_ref, cb_ref):
        up = _dot(f, w_ref[...])
        win_ref[z, 0:hr, :] = _dot(fh, w_ref[...]) * keep
        win_ref[z, hr:, :] = up
        y = cw_ref[taps - 1:taps, :] * up + cb_ref[...]
        for k in range(taps - 1):
            y = y + cw_ref[k:k + 1, :] * win_ref[z, pl.ds(hr - (taps - 1) + k, tm), :]
        return y

    g = conv(0, bg_ref, cwg_ref, cbg_ref)
    act = g * _sigmoid(g)
    v = conv(1, bv_ref, cwv_ref, cbv_ref)
    o_ref[...] = (act * v).astype(o_ref.dtype)


def _ffn_up(f, w_up, w_dw, b_dw, d_ff, tm, tn):
    m, d = f.shape
    taps = w_dw.shape[0]
    off = d_ff // tn
    per = tm // BF16_SUBLANES
    return pl.pallas_call(
        _ffn_up_kernel,
        grid=(d_ff // tn, m // tm),
        in_specs=[pl.BlockSpec((tm, d), lambda j, i: (i, 0)),
                  pl.BlockSpec((BF16_SUBLANES, d), lambda j, i: (jnp.maximum(i * per - 1, 0), 0)),
                  pl.BlockSpec((d, tn), lambda j, i: (0, j)),
                  pl.BlockSpec((d, tn), lambda j, i: (0, j + off)),
                  pl.BlockSpec((taps, tn), lambda j, i: (0, j)),
                  pl.BlockSpec((taps, tn), lambda j, i: (0, j + off)),
                  pl.BlockSpec((1, tn), lambda j, i: (0, j)),
                  pl.BlockSpec((1, tn), lambda j, i: (0, j + off))],
        out_specs=pl.BlockSpec((tm, tn), lambda j, i: (i, j)),
        out_shape=jax.ShapeDtypeStruct((m, d_ff), BF16),
        scratch_shapes=[pltpu.VMEM((2, tm + BF16_SUBLANES, tn), F32),
                        pltpu.VMEM((d, tn), BF16), pltpu.VMEM((d, tn), BF16)],
        compiler_params=_params("parallel", "arbitrary"),
        name="convffn_up",
    )(f, f, w_up, w_up, w_dw, w_dw, b_dw.reshape(1, -1), b_dw.reshape(1, -1))


def kernel(x, g_mix, w_in, w_conv_dw, b_conv_dw, ln_conv_g, ln_conv_b, w_proj_attn, w_proj_conv, w_out,
           g_ffn, w_up, w_ffn_dw, b_ffn_dw, w_down, g_final):
    b, s, d = x.shape
    assert b == 1, "single-sequence layer"
    attn_w = ATTN_HEADS * HEAD_DIM
    conv_c = w_conv_dw.shape[1]
    d_ff = w_down.shape[0]
    col_glu_a = 3 * attn_w
    col_glu_b = col_glu_a + conv_c
    col_ga = col_glu_b + conv_c
    col_gc = col_ga + d

    x2 = x.reshape(s, d)
    u = _rmsnorm(x2, g_mix, BF16, "rmsnorm_mix")
    qk = _matmul(u, w_in, 2 * attn_w, 0, *TILES["qk_proj"], BF16, "qk_proj")
    vt = _v_proj(u, w_in, 2 * attn_w, *TILES["v_proj"])
    c0 = _glu(u, w_in, col_glu_a, col_glu_b, conv_c, *TILES["glu_proj"])

    bias = _moba_gate(qk)
    attn = _moba_attention(qk, vt, bias)

    cact = _conv_ln_silu(c0, w_conv_dw, b_conv_dw, ln_conv_g, ln_conv_b)

    merged = _merge(u, attn, cact, w_in, col_ga, col_gc, w_proj_attn, w_proj_conv, *TILES["gated_merge"])
    h = _matmul(merged, w_out, d, 0, *TILES["out_proj"], F32, "out_proj", residual=x2)

    f = _rmsnorm(h, g_ffn, BF16, "rmsnorm_ffn")
    g = _ffn_up(f, w_up, w_ffn_dw, b_ffn_dw, d_ff, *TILES["convffn_up"])
    h2 = _matmul_rows_resident(g, w_down.astype(BF16), *TILES["ffn_down"], "ffn_down", residual=h)
    out = _rmsnorm(h2, g_final, F32, "rmsnorm_final")
    return out.reshape(b, s, d)
```

```python
import functools

import jax
import jax.numpy as jnp
import numpy as np
from jax import lax
from jax.experimental import pallas as pl
from jax.experimental.pallas import tpu as pltpu

ATTN_HEADS = 16
HEAD_DIM = 128
MOBA_BLOCK = 256
MOBA_TOPK = 3
NORM_EPS = 1e-6
LN_EPS = 1e-5
NEG_INF = -1e30

F32 = jnp.float32
BF16 = jnp.bfloat16

V7X_VMEM_BYTES = 64 * 1024 * 1024
VMEM_LIMIT_BYTES = V7X_VMEM_BYTES - 8 * 1024 * 1024
LANES = 128
F32_SUBLANES = 8
BF16_SUBLANES = 16
CONV_HALO_ROWS = 32
ONES_ROWS = 16

TILES = {
    "qk_proj": (1024, 512),
    "v_proj": (1024, 512),
    "glu_proj": (1024, 256),
    "gated_merge": (1024, 256),
    "out_proj": (1024, 512),
    "convffn_up": (1024, 256),
    "ffn_down": (512, 512),
}


def _params(*sem):
    return pltpu.CompilerParams(dimension_semantics=sem, vmem_limit_bytes=VMEM_LIMIT_BYTES)


def _dot(a, b):
    return jnp.dot(a, b, preferred_element_type=F32)


def _sigmoid(x):
    return 0.5 * jnp.tanh(0.5 * x) + 0.5


def _cast_weights_once(w_refs, wb_refs):
    @pl.when(pl.program_id(1) == 0)
    def _():
        for w_ref, wb_ref in zip(w_refs, wb_refs):
            wb_ref[...] = w_ref[...].astype(BF16)


def _rmsnorm_kernel(x_ref, g_ref, o_ref):
    x = x_ref[...]
    ms = jnp.mean(x * x, axis=-1, keepdims=True)
    o_ref[...] = (x * lax.rsqrt(ms + NORM_EPS) * g_ref[...]).astype(o_ref.dtype)


def _rmsnorm(x, g, out_dtype, name, rows=512):
    s, d = x.shape
    return pl.pallas_call(
        _rmsnorm_kernel,
        grid=(s // rows,),
        in_specs=[pl.BlockSpec((rows, d), lambda i: (i, 0)),
                  pl.BlockSpec((1, d), lambda i: (0, 0))],
        out_specs=pl.BlockSpec((rows, d), lambda i: (i, 0)),
        out_shape=jax.ShapeDtypeStruct((s, d), out_dtype),
        compiler_params=_params("parallel"),
        name=name,
    )(x, g.reshape(1, d))


def _mm_kernel(a_ref, w_ref, o_ref, wb_ref):
    _cast_weights_once([w_ref], [wb_ref])
    o_ref[...] = _dot(a_ref[...], wb_ref[...]).astype(o_ref.dtype)


def _mm_residual_kernel(a_ref, w_ref, r_ref, o_ref, wb_ref):
    _cast_weights_once([w_ref], [wb_ref])
    o_ref[...] = r_ref[...] + _dot(a_ref[...], wb_ref[...])


def _matmul(a, w, n, col0, tm, tn, out_dtype, name, residual=None):
    m, k = a.shape
    off = col0 // tn
    in_specs = [pl.BlockSpec((tm, k), lambda j, i: (i, 0)),
                pl.BlockSpec((k, tn), lambda j, i: (0, j + off))]
    args = [a, w]
    kern = _mm_kernel
    if residual is not None:
        in_specs.append(pl.BlockSpec((tm, tn), lambda j, i: (i, j)))
        args.append(residual)
        kern = _mm_residual_kernel
    return pl.pallas_call(
        kern,
        grid=(n // tn, m // tm),
        in_specs=in_specs,
        out_specs=pl.BlockSpec((tm, tn), lambda j, i: (i, j)),
        out_shape=jax.ShapeDtypeStruct((m, n), out_dtype),
        scratch_shapes=[pltpu.VMEM((k, tn), BF16)],
        compiler_params=_params("parallel", "arbitrary"),
        name=name,
    )(*args)


def _v_proj_kernel(a_ref, w_ref, o_ref, wb_ref):
    _cast_weights_once([w_ref], [wb_ref])
    acc = _dot(a_ref[...], wb_ref[...])
    n_heads, n_blocks = o_ref.shape[0], o_ref.shape[1]
    ones = jnp.ones((ONES_ROWS, MOBA_BLOCK), o_ref.dtype)
    for hh in range(n_heads):
        for rb in range(n_blocks):
            blk = acc[rb * MOBA_BLOCK:(rb + 1) * MOBA_BLOCK, hh * HEAD_DIM:(hh + 1) * HEAD_DIM]
            o_ref[hh, rb, 0:HEAD_DIM, :] = blk.T.astype(o_ref.dtype)
            o_ref[hh, rb, HEAD_DIM:, :] = ones


def _v_proj(a, w, col0, tm, tn):
    m, k = a.shape
    off = col0 // tn
    nb = m // MOBA_BLOCK
    return pl.pallas_call(
        _v_proj_kernel,
        grid=(ATTN_HEADS * HEAD_DIM // tn, m // tm),
        in_specs=[pl.BlockSpec((tm, k), lambda j, i: (i, 0)),
                  pl.BlockSpec((k, tn), lambda j, i: (0, j + off))],
        out_specs=pl.BlockSpec((tn // HEAD_DIM, tm // MOBA_BLOCK, HEAD_DIM + ONES_ROWS, MOBA_BLOCK),
                               lambda j, i: (j, i, 0, 0)),
        out_shape=jax.ShapeDtypeStruct((ATTN_HEADS, nb, HEAD_DIM + ONES_ROWS, MOBA_BLOCK), BF16),
        scratch_shapes=[pltpu.VMEM((k, tn), BF16)],
        compiler_params=_params("parallel", "arbitrary"),
        name="v_proj",
    )(a, w)


def _mm_bf16_residual_kernel(a_ref, w_ref, r_ref, o_ref):
    o_ref[...] = r_ref[...] + _dot(a_ref[...], w_ref[...])


def _matmul_rows_resident(a, w, tm, tn, name, residual):
    m, k = a.shape
    n = w.shape[1]
    return pl.pallas_call(
        _mm_bf16_residual_kernel,
        grid=(m // tm, n // tn),
        in_specs=[pl.BlockSpec((tm, k), lambda i, j: (i, 0)),
                  pl.BlockSpec((k, tn), lambda i, j: (0, j)),
                  pl.BlockSpec((tm, tn), lambda i, j: (i, j))],
        out_specs=pl.BlockSpec((tm, tn), lambda i, j: (i, j)),
        out_shape=jax.ShapeDtypeStruct((m, n), F32),
        compiler_params=_params("parallel", "arbitrary"),
        name=name,
    )(a, w, residual)


def _glu_kernel(u_ref, wa_ref, wb_ref, o_ref, wab_ref, wbb_ref):
    _cast_weights_once([wa_ref, wb_ref], [wab_ref, wbb_ref])
    u = u_ref[...]
    o_ref[...] = _dot(u, wab_ref[...]) * _sigmoid(_dot(u, wbb_ref[...]))


def _glu(u, w_in, col_a, col_b, n, tm, tn):
    m, k = u.shape
    oa, ob = col_a // tn, col_b // tn
    return pl.pallas_call(
        _glu_kernel,
        grid=(n // tn, m // tm),
        in_specs=[pl.BlockSpec((tm, k), lambda j, i: (i, 0)),
                  pl.BlockSpec((k, tn), lambda j, i: (0, j + oa)),
                  pl.BlockSpec((k, tn), lambda j, i: (0, j + ob))],
        out_specs=pl.BlockSpec((tm, tn), lambda j, i: (i, j)),
        out_shape=jax.ShapeDtypeStruct((m, n), F32),
        scratch_shapes=[pltpu.VMEM((k, tn), BF16), pltpu.VMEM((k, tn), BF16)],
        compiler_params=_params("parallel", "arbitrary"),
        name="glu_proj",
    )(u, w_in, w_in)


def _moba_gate_kernel(q_ref, k_ref, o_ref, *, tq):
    s = k_ref.shape[0]
    nb = s // MOBA_BLOCK
    blk = lax.broadcasted_iota(jnp.int32, (nb, tq), 0)
    qpos = pl.program_id(1) * tq + lax.broadcasted_iota(jnp.int32, (nb, tq), 1)
    past = blk < qpos // MOBA_BLOCK
    for a in range(o_ref.shape[0]):
        cols = slice(a * HEAD_DIM, (a + 1) * HEAD_DIM)
        k = k_ref[:, cols].astype(F32).reshape(nb, MOBA_BLOCK, HEAD_DIM)
        k_mean = jnp.sum(k, axis=1) * (1.0 / MOBA_BLOCK)
        q = q_ref[:, cols]
        gate = jnp.zeros((nb, tq), F32)
        rest = k_mean
        for _ in range(3):
            part = rest.astype(BF16)
            gate = gate + lax.dot_general(part, q, (((1,), (1,)), ((), ())), preferred_element_type=F32)
            rest = rest - part.astype(F32)
        gate = jnp.where(past, gate, NEG_INF)
        picked = jnp.zeros((nb, tq), jnp.bool_)
        for _ in range(min(MOBA_TOPK, nb)):
            top = jnp.max(gate, axis=0, keepdims=True)
            first = jnp.min(jnp.where(gate == top, blk, nb), axis=0, keepdims=True)
            hit = blk == first
            picked = jnp.logical_or(picked, hit)
            gate = jnp.where(hit, -jnp.inf, gate)
        o_ref[a] = jnp.where(jnp.logical_and(picked, past), 0.0, NEG_INF)


def _moba_gate(qk, tq=4096, heads_per_step=4):
    s = qk.shape[0]
    nb = s // MOBA_BLOCK
    groups = ATTN_HEADS // heads_per_step
    width = heads_per_step * HEAD_DIM
    return pl.pallas_call(
        functools.partial(_moba_gate_kernel, tq=tq),
        grid=(groups, s // tq),
        in_specs=[pl.BlockSpec((tq, width), lambda h, i: (i, h)),
                  pl.BlockSpec((s, width), lambda h, i: (0, groups + h))],
        out_specs=pl.BlockSpec((heads_per_step, nb, tq), lambda h, i: (h, 0, i)),
        out_shape=jax.ShapeDtypeStruct((ATTN_HEADS, nb, s), F32),
        compiler_params=_params("parallel", "arbitrary"),
        name="moba_gate",
    )(qk, qk)


SCORE_GROUP = 8
PROB_GROUP = 4
HEADS_PER_STEP = 4
LOG2E = 1.4426950408889634


def _moba_attn_kernel(zslope_ref, q_ref, k_ref, kb_ref, vt_ref, bias_ref, o_ref, s_ref):
    qb = pl.program_id(1)
    bs = MOBA_BLOCK
    hd = HEAD_DIM
    c = (hd ** -0.5) * LOG2E
    heads = range(HEADS_PER_STEP)
    zslope = [zslope_ref[pl.program_id(0) * HEADS_PER_STEP + a] for a in heads]
    nt = (((1,), (1,)), ((), ()))
    ones3 = (lax.broadcasted_iota(jnp.int32, (bs, hd), 1) < 3).astype(BF16)
    q2 = [jnp.concatenate([q_ref[:, a * hd:(a + 1) * hd], ones3], axis=1) for a in heads]

    def scores(a, row0, n_rows):
        rows = pl.ds(pl.multiple_of(row0, bs), n_rows)
        k2 = jnp.concatenate([k_ref[rows, a * hd:(a + 1) * hd], kb_ref[a, 0:n_rows, :]], axis=1)
        return lax.dot_general(k2, q2[a], nt, preferred_element_type=F32)

    def pick(a, kb):
        return zslope[a] * (kb - qb).astype(F32) + bias_ref[a, pl.ds(kb, 1), :]

    causal = lax.broadcasted_iota(jnp.int32, (bs, bs), 0) <= lax.broadcasted_iota(jnp.int32, (bs, bs), 1)
    z_own = [jnp.where(causal, c * scores(a, qb * bs, bs), NEG_INF) for a in heads]
    m = tuple(jnp.max(z_own[a], axis=0, keepdims=True) for a in heads)

    def span_loop(span, group, carry):
        n_full = qb // group
        rest = qb - n_full * group
        carry = lax.fori_loop(0, n_full, lambda g, cr: span(g * group, group, cr), carry)
        return lax.cond(
            rest > group // 2,
            lambda cr: span(n_full * group, group, cr),
            lambda cr: lax.cond(rest > 0, lambda c2: span(n_full * group, group // 2, c2), lambda c2: c2, cr),
            carry)

    def pass1(kb0, n_blocks, m):
        m = list(m)
        for a in heads:
            z = c * scores(a, kb0 * bs, n_blocks * bs)
            s_ref[a, pl.ds(pl.multiple_of(kb0 * bs, bs), n_blocks * bs), :] = z
            for b in range(n_blocks):
                top = jnp.max(z[b * bs:(b + 1) * bs], axis=0, keepdims=True)
                m[a] = jnp.maximum(m[a], top + pick(a, kb0 + b))
        return tuple(m)

    m = span_loop(pass1, SCORE_GROUP, m)

    acc = tuple(_dot(vt_ref[a, qb], jnp.exp2(z_own[a] - m[a]).astype(BF16)) for a in heads)

    def pass2(kb0, n_blocks, acc):
        acc = list(acc)
        for b in range(n_blocks):
            kb = kb0 + b
            for a in heads:
                z = s_ref[a, pl.ds(pl.multiple_of(kb * bs, bs), bs), :]
                w = pick(a, kb) - m[a]
                acc[a] = acc[a] + _dot(vt_ref[a, kb], jnp.exp2(z + w).astype(BF16))
        return tuple(acc)

    acc = span_loop(pass2, PROB_GROUP, acc)
    for a in heads:
        o_ref[:, a * hd:(a + 1) * hd] = (acc[a][0:hd] / acc[a][hd:hd + 1]).T.astype(o_ref.dtype)


def _moba_attention(qk, vt, bias):
    s = qk.shape[0]
    nb = s // MOBA_BLOCK
    assert nb % SCORE_GROUP == 0 and SCORE_GROUP % PROB_GROUP == 0
    scale = HEAD_DIM ** -0.5
    slopes = np.exp2(-8.0 * np.arange(1, ATTN_HEADS + 1, dtype=np.float32) / ATTN_HEADS).astype(np.float32)
    zslope = jnp.asarray(slopes * np.float32(LOG2E * MOBA_BLOCK))
    rest = (slopes[:, None] * np.arange(MOBA_BLOCK, dtype=np.float32)[None, :] / np.float32(scale)).astype(np.float32)
    cols = []
    for _ in range(3):
        part = rest.astype(BF16)
        cols.append(part)
        rest = rest - part.astype(np.float32)
    kbias = np.zeros((ATTN_HEADS, MOBA_BLOCK, HEAD_DIM), BF16)
    kbias[:, :, 0:3] = np.stack(cols, axis=-1)
    kbias = jnp.asarray(np.tile(kbias, (1, SCORE_GROUP, 1)))
    hps = HEADS_PER_STEP
    groups = ATTN_HEADS // hps
    once = pl.Buffered(1)
    return pl.pallas_call(
        _moba_attn_kernel,
        grid=(groups, nb),
        in_specs=[pl.BlockSpec(memory_space=pltpu.SMEM),
                  pl.BlockSpec((MOBA_BLOCK, hps * HEAD_DIM), lambda h, i: (i, h)),
                  pl.BlockSpec((s, hps * HEAD_DIM), lambda h, i: (0, groups + h), pipeline_mode=once),
                  pl.BlockSpec((hps, SCORE_GROUP * MOBA_BLOCK, HEAD_DIM), lambda h, i: (h, 0, 0), pipeline_mode=once),
                  pl.BlockSpec((hps, nb, HEAD_DIM + ONES_ROWS, MOBA_BLOCK), lambda h, i: (h, 0, 0, 0),
                               pipeline_mode=once),
                  pl.BlockSpec((hps, nb, MOBA_BLOCK), lambda h, i: (h, 0, i))],
        out_specs=pl.BlockSpec((MOBA_BLOCK, hps * HEAD_DIM), lambda h, i: (i, h)),
        out_shape=jax.ShapeDtypeStruct((s, ATTN_HEADS * HEAD_DIM), BF16),
        scratch_shapes=[pltpu.VMEM((hps, s, MOBA_BLOCK), F32)],
        compiler_params=_params("parallel", "arbitrary"),
        name="moba_attention",
    )(zslope, qk, qk, kbias, vt, bias)


def _conv_ln_silu_kernel(x_ref, halo_ref, w_ref, b_ref, g_ref, beta_ref, o_ref, win_ref, y_ref, *, taps):
    i = pl.program_id(0)
    j = pl.program_id(1)
    nj = y_ref.shape[0]
    t, ct = x_ref.shape
    hr = halo_ref.shape[0]
    win_ref[0, 0:hr, :] = jnp.where(i == 0, 0.0, halo_ref[...])
    win_ref[0, hr:, :] = x_ref[...]
    span = hr + t - F32_SUBLANES
    for p in range(1, F32_SUBLANES):
        win_ref[p, 0:span, :] = win_ref[0, pl.ds(p, span), :]
    for c in range(ct // LANES):
        lanes = slice(c * LANES, (c + 1) * LANES)
        acc = jnp.broadcast_to(b_ref[:, lanes], (t, LANES))
        for k in range(taps):
            a, p = divmod(hr - (taps - 1) + k, F32_SUBLANES)
            acc = acc + w_ref[k:k + 1, lanes] * win_ref[p, F32_SUBLANES * a:F32_SUBLANES * a + t, lanes]
        y_ref[j, :, lanes] = acc

    @pl.when(j == nj - 1)
    def _():
        n = nj * ct
        mu = sum(jnp.sum(y_ref[jj], axis=-1, keepdims=True) for jj in range(nj)) * (1.0 / n)
        var = sum(jnp.sum(jnp.square(y_ref[jj] - mu), axis=-1, keepdims=True) for jj in range(nj)) * (1.0 / n)
        inv = lax.rsqrt(var + LN_EPS)
        for jj in range(nj):
            cols = slice(jj * ct, (jj + 1) * ct)
            y = (y_ref[jj] - mu) * inv * g_ref[:, cols] + beta_ref[:, cols]
            o_ref[:, cols] = (y * _sigmoid(y)).astype(o_ref.dtype)


def _conv_ln_silu(x, w, b, g, beta, rows=256, cols=512):
    s, c = x.shape
    taps = w.shape[0]
    assert taps - 1 <= CONV_HALO_ROWS
    per = rows // CONV_HALO_ROWS
    nj = c // cols
    return pl.pallas_call(
        functools.partial(_conv_ln_silu_kernel, taps=taps),
        grid=(s // rows, nj),
        in_specs=[pl.BlockSpec((rows, cols), lambda i, j: (i, j)),
                  pl.BlockSpec((CONV_HALO_ROWS, cols), lambda i, j: (jnp.maximum(i * per - 1, 0), j)),
                  pl.BlockSpec((taps, cols), lambda i, j: (0, j)),
                  pl.BlockSpec((1, cols), lambda i, j: (0, j)),
                  pl.BlockSpec((1, c), lambda i, j: (0, 0)),
                  pl.BlockSpec((1, c), lambda i, j: (0, 0))],
        out_specs=pl.BlockSpec((rows, c), lambda i, j: (i, 0)),
        out_shape=jax.ShapeDtypeStruct((s, c), BF16),
        scratch_shapes=[pltpu.VMEM((F32_SUBLANES, rows + CONV_HALO_ROWS, cols), F32),
                        pltpu.VMEM((nj, rows, cols), F32)],
        compiler_params=_params("parallel", "arbitrary"),
        name="conformer_conv_ln_silu",
    )(x, x, w, b.reshape(1, c), g.reshape(1, c), beta.reshape(1, c))


def _merge_kernel(u_ref, a_ref, c_ref, win_hbm, wpa_hbm, wpc_hbm, o_ref,
                  sga_ref, sgc_ref, spa_ref, spc_ref, bga_ref, bgc_ref, bpa_ref, bpc_ref, sem, *, col_ga, col_gc):
    j = pl.program_id(0)
    nj = pl.num_programs(0)
    tn = o_ref.shape[1]

    def weight_copies(jj):
        col = pl.multiple_of(jj * tn, tn)
        return (pltpu.make_async_copy(win_hbm.at[:, pl.ds(col_ga + col, tn)], sga_ref, sem.at[0]),
                pltpu.make_async_copy(win_hbm.at[:, pl.ds(col_gc + col, tn)], sgc_ref, sem.at[1]),
                pltpu.make_async_copy(wpa_hbm.at[:, pl.ds(col, tn)], spa_ref, sem.at[2]),
                pltpu.make_async_copy(wpc_hbm.at[:, pl.ds(col, tn)], spc_ref, sem.at[3]))

    @pl.when(pl.program_id(1) == 0)
    def _():
        @pl.when(j == 0)
        def _():
            for cp in weight_copies(0):
                cp.start()

        for cp in weight_copies(j):
            cp.wait()
        for s_ref, b_ref in ((sga_ref, bga_ref), (sgc_ref, bgc_ref), (spa_ref, bpa_ref), (spc_ref, bpc_ref)):
            b_ref[...] = s_ref[...].astype(BF16)

        @pl.when(j + 1 < nj)
        def _():
            for cp in weight_copies(j + 1):
                cp.start()

    u = u_ref[...]
    ga = _sigmoid(_dot(u, bga_ref[...]))
    gc = _sigmoid(_dot(u, bgc_ref[...]))
    o_ref[...] = (ga * _dot(a_ref[...], bpa_ref[...]) + gc * _dot(c_ref[...], bpc_ref[...])).astype(o_ref.dtype)


def _merge(u, attn, cact, w_in, col_ga, col_gc, w_pa, w_pc, tm, tn):
    m, d = u.shape
    n = w_pa.shape[1]
    ka, kc = w_pa.shape[0], w_pc.shape[0]
    hbm = pl.BlockSpec(memory_space=pl.ANY)
    return pl.pallas_call(
        functools.partial(_merge_kernel, col_ga=col_ga, col_gc=col_gc),
        grid=(n // tn, m // tm),
        in_specs=[pl.BlockSpec((tm, d), lambda j, i: (i, 0)),
                  pl.BlockSpec((tm, attn.shape[1]), lambda j, i: (i, 0)),
                  pl.BlockSpec((tm, cact.shape[1]), lambda j, i: (i, 0)),
                  hbm, hbm, hbm],
        out_specs=pl.BlockSpec((tm, tn), lambda j, i: (i, j)),
        out_shape=jax.ShapeDtypeStruct((m, n), BF16),
        scratch_shapes=[pltpu.VMEM((d, tn), F32), pltpu.VMEM((d, tn), F32),
                        pltpu.VMEM((ka, tn), F32), pltpu.VMEM((kc, tn), F32),
                        pltpu.VMEM((d, tn), BF16), pltpu.VMEM((d, tn), BF16),
                        pltpu.VMEM((ka, tn), BF16), pltpu.VMEM((kc, tn), BF16),
                        pltpu.SemaphoreType.DMA((4,))],
        compiler_params=pltpu.CompilerParams(dimension_semantics=("arbitrary", "arbitrary"),
                                             vmem_limit_bytes=V7X_VMEM_BYTES - 2 * 1024 * 1024),
        name="gated_merge",
    )(u, attn, cact, w_in, w_pa, w_pc)


def _ffn_up_kernel(f_ref, fh_ref, wg_ref, wv_ref, cwg_ref, cwv_ref, cbg_ref, cbv_ref, o_ref, win_ref, bg_ref, bv_ref):
    _cast_weights_once([wg_ref, wv_ref], [bg_ref, bv_ref])
    tm = f_ref.shape[0]
    hr = fh_ref.shape[0]
    taps = cwg_ref.shape[0]
    keep = (pl.program_id(1) > 0).astype(F32)
    f = f_ref[...]
    fh = fh_ref[...]

    def conv(z, w_ref, cw_ref, cb_ref):
        up = _dot(f, w_ref[...])
        win_ref[z, 0:hr, :] = _dot(fh, w_ref[...]) * keep
        win_ref[z, hr:, :] = up
        y = cw_ref[taps - 1:taps, :] * up + cb_ref[...]
        for k in range(taps - 1):
            y = y + cw_ref[k:k + 1, :] * win_ref[z, pl.ds(hr - (taps - 1) + k, tm), :]
        return y

    g = conv(0, bg_ref, cwg_ref, cbg_ref)
    act = g * _sigmoid(g)
    v = conv(1, bv_ref, cwv_ref, cbv_ref)
    o_ref[...] = (act * v).astype(o_ref.dtype)


def _ffn_up(f, w_up, w_dw, b_dw, d_ff, tm, tn):
    m, d = f.shape
    taps = w_dw.shape[0]
    off = d_ff // tn
    per = tm // BF16_SUBLANES
    return pl.pallas_call(
        _ffn_up_kernel,
        grid=(d_ff // tn, m // tm),
        in_specs=[pl.BlockSpec((tm, d), lambda j, i: (i, 0)),
                  pl.BlockSpec((BF16_SUBLANES, d), lambda j, i: (jnp.maximum(i * per - 1, 0), 0)),
                  pl.BlockSpec((d, tn), lambda j, i: (0, j)),
                  pl.BlockSpec((d, tn), lambda j, i: (0, j + off)),
                  pl.BlockSpec((taps, tn), lambda j, i: (0, j)),
                  pl.BlockSpec((taps, tn), lambda j, i: (0, j + off)),
                  pl.BlockSpec((1, tn), lambda j, i: (0, j)),
                  pl.BlockSpec((1, tn), lambda j, i: (0, j + off))],
        out_specs=pl.BlockSpec((tm, tn), lambda j, i: (i, j)),
        out_shape=jax.ShapeDtypeStruct((m, d_ff), BF16),
        scratch_shapes=[pltpu.VMEM((2, tm + BF16_SUBLANES, tn), F32),
                        pltpu.VMEM((d, tn), BF16), pltpu.VMEM((d, tn), BF16)],
        compiler_params=_params("parallel", "arbitrary"),
        name="convffn_up",
    )(f, f, w_up, w_up, w_dw, w_dw, b_dw.reshape(1, -1), b_dw.reshape(1, -1))


def kernel(x, g_mix, w_in, w_conv_dw, b_conv_dw, ln_conv_g, ln_conv_b, w_proj_attn, w_proj_conv, w_out,
           g_ffn, w_up, w_ffn_dw, b_ffn_dw, w_down, g_final):
    b, s, d = x.shape
    assert b == 1, "single-sequence layer"
    attn_w = ATTN_HEADS * HEAD_DIM
    conv_c = w_conv_dw.shape[1]
    d_ff = w_down.shape[0]
    col_glu_a = 3 * attn_w
    col_glu_b = col_glu_a + conv_c
    col_ga = col_glu_b + conv_c
    col_gc = col_ga + d

    x2 = x.reshape(s, d)
    u = _rmsnorm(x2, g_mix, BF16, "rmsnorm_mix")
    qk = _matmul(u, w_in, 2 * attn_w, 0, *TILES["qk_proj"], BF16, "qk_proj")
    vt = _v_proj(u, w_in, 2 * attn_w, *TILES["v_proj"])
    c0 = _glu(u, w_in, col_glu_a, col_glu_b, conv_c, *TILES["glu_proj"])

    bias = _moba_gate(qk)
    attn = _moba_attention(qk, vt, bias)

    cact = _conv_ln_silu(c0, w_conv_dw, b_conv_dw, ln_conv_g, ln_conv_b)

    merged = _merge(u, attn, cact, w_in, col_ga, col_gc, w_proj_attn, w_proj_conv, *TILES["gated_merge"])
    h = _matmul(merged, w_out, d, 0, *TILES["out_proj"], F32, "out_proj", residual=x2)

    f = _rmsnorm(h, g_ffn, BF16, "rmsnorm_ffn")
    g = _ffn_up(f, w_up, w_ffn_dw, b_ffn_dw, d_ff, *TILES["convffn_up"])
    h2 = _matmul_rows_resident(g, w_down.astype(BF16), *TILES["ffn_down"], "ffn_down", residual=h)
    out = _rmsnorm(h2, g_final, F32, "rmsnorm_final")
    return out.reshape(b, s, d)
```
